```python
import math
import jax
import jax.numpy as jnp
from jax import lax
import numpy as np

D_MODEL = 1024
BATCH = 1
SEQ = 16384
DEPTH = 2

HEAD_DIM = 64
GDN_HEADS = 8
RWKV_HEADS = 8
HGRN_HEADS = 8
GDN_WIDTH = GDN_HEADS * HEAD_DIM
RWKV_WIDTH = RWKV_HEADS * HEAD_DIM
HGRN_EXPAND = 64
HGRN_FDIM = HGRN_HEADS * HGRN_EXPAND
HGRN_WIDTH = HGRN_HEADS * HEAD_DIM
D_MIX = GDN_WIDTH + RWKV_WIDTH + HGRN_WIDTH
CONV_WIDTH = 4
RWKV_DECAY_LORA = 64
RWKV_ICLR_LORA = 64
CHUNK = 64
NORM_EPS = 1e-6
L2_EPS = 1e-6
RWKV_GN_EPS = 64e-5

GDN_SPLIT = (GDN_WIDTH, GDN_WIDTH, GDN_WIDTH, GDN_HEADS, GDN_HEADS, GDN_WIDTH)
RWKV_SPLIT = (RWKV_WIDTH, RWKV_WIDTH, RWKV_WIDTH, RWKV_DECAY_LORA, RWKV_ICLR_LORA, RWKV_WIDTH)
HGRN_SPLIT = (HGRN_FDIM, HGRN_FDIM, HGRN_WIDTH, HGRN_WIDTH)
GDN_PROJ = sum(GDN_SPLIT)
RWKV_PROJ = sum(RWKV_SPLIT)
HGRN_PROJ = sum(HGRN_SPLIT)
PROJ_DIM = GDN_PROJ + RWKV_PROJ + HGRN_PROJ

kernel_name = 'hybrid_gdn_rwkv7_hgrn2_parallel_heads'

F32 = jnp.float32


def _split(t, sizes):
    out, start = [], 0
    for s in sizes:
        out.append(t[..., start:start + s])
        start += s
    return out


def _heads(t, h):
    return t.reshape(t.shape[:-1] + (h, t.shape[-1] // h))


def _rmsnorm(x, w):
    xf = x.astype(F32)
    y = xf * lax.rsqrt(jnp.mean(xf * xf, axis=-1, keepdims=True) + NORM_EPS)
    return (y * w.astype(F32)).astype(x.dtype)


def _l2norm(x):
    return x * lax.rsqrt(jnp.sum(x * x, axis=-1, keepdims=True) + L2_EPS)


def _causal_conv(x, w):
    return lax.conv_general_dilated(x, w[:, None, :], window_strides=(1,),
                                    padding=[(CONV_WIDTH - 1, 0)],
                                    dimension_numbers=('NWC', 'WIO', 'NWC'),
                                    feature_group_count=x.shape[-1])


def _token_shift(p, mu):
    prev = jnp.pad(p, ((0, 0), (1, 0), (0, 0)))[:, :-1]
    return p + mu * (prev - p)


def _gdn_chunked(q, k, v, beta, g):
    B, T, H, Dk = q.shape
    Dv = v.shape[-1]
    N = T // CHUNK

    def to_chunks(t):
        t = t.reshape((B, N, CHUNK, H) + t.shape[3:])
        return jnp.moveaxis(t, 3, 1)

    q, k, v, beta, g = (to_chunks(t) for t in (q, k, v, beta, g))
    gc = jnp.cumsum(g, axis=-1)
    causal = jnp.tril(jnp.ones((CHUNK, CHUNK), bool))
    strict = jnp.tril(jnp.ones((CHUNK, CHUNK), bool), -1)
    diff = gc[..., :, None] - gc[..., None, :]
    decay_mat = jnp.where(causal, jnp.exp(jnp.where(causal, diff, 0.0)), 0.0)
    k_beta = k * beta[..., None]
    A = jnp.where(strict, jnp.einsum('bhnik,bhnjk->bhnij', k_beta, k) * decay_mat, 0.0)
    eye = jnp.eye(CHUNK, dtype=A.dtype)
    Tm = lax.linalg.triangular_solve(eye + A, jnp.broadcast_to(eye, A.shape),
                                     left_side=True, lower=True)
    u = jnp.einsum('bhnij,bhnjd->bhnid', Tm, v * beta[..., None])
    w = jnp.einsum('bhnij,bhnjd->bhnid', Tm, k_beta * jnp.exp(gc)[..., None])
    qk = jnp.einsum('bhnik,bhnjk->bhnij', q, k) * decay_mat
    q_dec = q * jnp.exp(gc)[..., None]
    k_dec = k * jnp.exp(gc[..., -1:] - gc)[..., None]
    last = jnp.exp(gc[..., -1])

    def step(S, xs):
        u_n, w_n, qk_n, qd_n, kd_n, last_n = xs
        v_new = u_n - jnp.einsum('bhck,bhkv->bhcv', w_n, S)
        o = jnp.einsum('bhck,bhkv->bhcv', qd_n, S) + jnp.einsum('bhij,bhjv->bhiv', qk_n, v_new)
        S = S * last_n[..., None, None] + jnp.einsum('bhck,bhcv->bhkv', kd_n, v_new)
        return S, o

    xs = tuple(jnp.moveaxis(t, 2, 0) for t in (u, w, qk, q_dec, k_dec, last))
    S0 = jnp.zeros((B, H, Dk, Dv), q.dtype)
    _, o = lax.scan(step, S0, xs)
    o = jnp.moveaxis(o, 0, 2)
    return jnp.moveaxis(o, 1, 3).reshape(B, T, H, Dv)


def _gdn_branch(p, conv_w, a_log, dt_bias, norm_w):
    q, k, v, b_raw, a_raw, gate = _split(p, GDN_SPLIT)
    qkv = jax.nn.silu(_causal_conv(jnp.concatenate([q, k, v], axis=-1), conv_w.astype(F32)))
    q, k, v = _split(qkv, (GDN_WIDTH, GDN_WIDTH, GDN_WIDTH))
    q = _l2norm(_heads(q, GDN_HEADS)) * (HEAD_DIM ** -0.5)
    k = _l2norm(_heads(k, GDN_HEADS))
    v = _heads(v, GDN_HEADS)
    beta = jax.nn.sigmoid(b_raw)
    g = -jnp.exp(a_log.astype(F32)) * jax.nn.softplus(a_raw + dt_bias.astype(F32))
    o = _gdn_chunked(q, k, v, beta, g)
    o = _rmsnorm(o, norm_w).reshape(o.shape[:2] + (GDN_WIDTH,))
    return o * jax.nn.silu(gate)


def _rwkv7_scan(r, decay, k, v, kk, a):
    B, T, H, D = r.shape

    def step(S, xs):
        r_t, w_t, k_t, v_t, kk_t, a_t = xs
        sa = jnp.einsum('bhvk,bhk->bhv', S, -kk_t)
        S = (S * w_t[:, :, None, :] + sa[..., :, None] * (kk_t * a_t)[..., None, :]
             + v_t[..., :, None] * k_t[..., None, :])
        return S, jnp.einsum('bhvk,bhk->bhv', S, r_t)

    xs = tuple(jnp.moveaxis(t, 1, 0) for t in (r, decay, k, v, kk, a))
    S0 = jnp.zeros((B, H, D, D), r.dtype)
    _, o = lax.scan(step, S0, xs)
    return jnp.moveaxis(o, 0, 1)


def _rwkv7_branch(p, mu, w0, w_up, a0, a_up, k_k, k_a, r_k, ln_w, ln_b):
    p = _token_shift(p, mu.astype(F32))
    r, k, v, wd, ad, gate = _split(p, RWKV_SPLIT)
    w_raw = -jax.nn.softplus(-(w0.astype(F32) + jnp.tanh(wd) @ w_up.astype(F32))) - 0.5
    decay = jnp.exp(-jnp.exp(w_raw))
    a = jax.nn.sigmoid(a0.astype(F32) + ad @ a_up.astype(F32))
    kk = _l2norm(_heads(k * k_k.astype(F32), RWKV_HEADS))
    k = k * (1.0 + (a - 1.0) * k_a.astype(F32))
    rh, kh, vh = _heads(r, RWKV_HEADS), _heads(k, RWKV_HEADS), _heads(v, RWKV_HEADS)
    o = _rwkv7_scan(rh, _heads(decay, RWKV_HEADS), kh, vh, kk, _heads(a, RWKV_HEADS))
    mean = jnp.mean(o, axis=-1, keepdims=True)
    var = jnp.mean(jnp.square(o - mean), axis=-1, keepdims=True)
    o = (o - mean) * lax.rsqrt(var + RWKV_GN_EPS)
    o = o.reshape(o.shape[:2] + (RWKV_WIDTH,)) * ln_w.astype(F32) + ln_b.astype(F32)
    bonus = jnp.sum(rh * kh * r_k.astype(F32), axis=-1, keepdims=True) * vh
    o = o + bonus.reshape(o.shape)
    return o * jax.nn.silu(gate)


def _hgrn2_chunked(q, k, v, log_g):
    B, T, H, Dk = q.shape
    Dv = v.shape[-1]
    N = T // CHUNK

    def to_chunks(t):
        return t.reshape(B, N, CHUNK, H, t.shape[-1]).transpose(1, 0, 3, 2, 4)

    q, k, v, lg = (to_chunks(t) for t in (q, k, v, log_g))
    bcum = jnp.cumsum(lg, axis=3)
    causal = jnp.tril(jnp.ones((CHUNK, CHUNK), bool))[:, :, None]

    def step(S, xs):
        q_n, k_n, v_n, b_n = xs
        diff = b_n[..., :, None, :] - b_n[..., None, :, :]
        dec = jnp.where(causal, jnp.exp(jnp.where(causal, diff, 0.0)), 0.0)
        att = jnp.einsum('bhik,bhijk,bhjk->bhij', q_n, dec, k_n)
        o = (jnp.einsum('bhij,bhjv->bhiv', att, v_n)
             + jnp.einsum('bhik,bhkv->bhiv', q_n * jnp.exp(b_n), S))
        b_last = b_n[..., -1:, :]
        S = (S * jnp.exp(b_last)[..., 0, :, None]
             + jnp.einsum('bhjk,bhjv->bhkv', k_n * jnp.exp(b_last - b_n), v_n))
        return S, o

    S0 = jnp.zeros((B, H, Dk, Dv), q.dtype)
    _, o = lax.scan(step, S0, (q, k, v, bcum))
    return o.transpose(1, 0, 3, 2, 4).reshape(B, T, H, Dv)


def _hgrn2_branch(p, lb, norm_w):
    q, f, i, gate = _split(p, HGRN_SPLIT)
    q = jax.nn.silu(_heads(q, HGRN_HEADS))
    lb = _heads(lb, HGRN_HEADS)
    f = _heads(f, HGRN_HEADS)
    log_g = jax.nn.log_sigmoid(f) + jnp.log1p(lb * jnp.exp(-f))
    k = (1.0 - lb) * jax.nn.sigmoid(-f)
    o = _hgrn2_chunked(q, k, _heads(i, HGRN_HEADS), log_g)
    o = _rmsnorm(o.reshape(o.shape[:2] + (HGRN_WIDTH,)), norm_w)
    return o * jax.nn.silu(gate)


def setup_inputs(seed: int = 0) -> dict:
    key = jax.random.key(seed)
    ks = jax.random.split(key, 24)

    def nrm(k, shape, s):
        return s * jax.random.normal(k, shape, F32)

    x = jax.random.normal(ks[0], (BATCH, SEQ, D_MODEL), F32)
    norm_w = 1.0 + nrm(ks[1], (DEPTH, D_MODEL), 0.02)
    w_in = nrm(ks[2], (DEPTH, D_MODEL, PROJ_DIM), D_MODEL ** -0.5)
    gdn_conv_w = nrm(ks[3], (DEPTH, CONV_WIDTH, 3 * GDN_WIDTH), CONV_WIDTH ** -0.5)
    gdn_a_log = jnp.log(jax.random.uniform(ks[4], (DEPTH, GDN_HEADS), F32, 1.0, 16.0))
    dt = jnp.exp(jax.random.uniform(ks[5], (DEPTH, GDN_HEADS), F32, math.log(1e-3), math.log(1e-1)))
    gdn_dt_bias = dt + jnp.log(-jnp.expm1(-dt))
    gdn_norm_w = 1.0 + nrm(ks[6], (DEPTH, HEAD_DIM), 0.02)
    rwkv_mu = jax.random.uniform(ks[7], (DEPTH, RWKV_PROJ), F32, 0.0, 1.0)
    rwkv_w0 = jax.random.uniform(ks[8], (DEPTH, RWKV_WIDTH), F32, -6.0, -1.0)
    rwkv_w_up = nrm(ks[9], (DEPTH, RWKV_DECAY_LORA, RWKV_WIDTH), 0.5 * RWKV_DECAY_LORA ** -0.5)
    rwkv_a0 = nrm(ks[10], (DEPTH, RWKV_WIDTH), 0.1)
    rwkv_a_up = nrm(ks[11], (DEPTH, RWKV_ICLR_LORA, RWKV_WIDTH), 0.5 * RWKV_ICLR_LORA ** -0.5)
    rwkv_k_k = 0.85 + nrm(ks[12], (DEPTH, RWKV_WIDTH), 0.02)
    rwkv_k_a = 1.0 + nrm(ks[13], (DEPTH, RWKV_WIDTH), 0.02)
    rwkv_r_k = -0.04 + nrm(ks[14], (DEPTH, RWKV_HEADS, HEAD_DIM), 0.1)
    rwkv_ln_w = 1.0 + nrm(ks[15], (DEPTH, RWKV_WIDTH), 0.02)
    rwkv_ln_b = nrm(ks[16], (DEPTH, RWKV_WIDTH), 0.02)
    hgrn_lower_bounds = nrm(ks[17], (DEPTH, HGRN_FDIM), 0.5)
    hgrn_norm_w = 1.0 + nrm(ks[18], (DEPTH, HGRN_WIDTH), 0.02)
    w_out = nrm(ks[19], (DEPTH, D_MIX, D_MODEL), D_MIX ** -0.5)
    final_norm_w = 1.0 + nrm(ks[20], (D_MODEL,), 0.02)
    return {'x': x, 'norm_w': norm_w, 'w_in': w_in, 'gdn_conv_w': gdn_conv_w,
            'gdn_a_log': gdn_a_log, 'gdn_dt_bias': gdn_dt_bias, 'gdn_norm_w': gdn_norm_w,
            'rwkv_mu': rwkv_mu, 'rwkv_w0': rwkv_w0, 'rwkv_w_up': rwkv_w_up, 'rwkv_a0': rwkv_a0,
            'rwkv_a_up': rwkv_a_up, 'rwkv_k_k': rwkv_k_k, 'rwkv_k_a': rwkv_k_a, 'rwkv_r_k': rwkv_r_k,
            'rwkv_ln_w': rwkv_ln_w, 'rwkv_ln_b': rwkv_ln_b, 'hgrn_lower_bounds': hgrn_lower_bounds,
            'hgrn_norm_w': hgrn_norm_w, 'w_out': w_out, 'final_norm_w': final_norm_w}


def reference(x, norm_w, w_in, gdn_conv_w, gdn_a_log, gdn_dt_bias, gdn_norm_w,
              rwkv_mu, rwkv_w0, rwkv_w_up, rwkv_a0, rwkv_a_up, rwkv_k_k, rwkv_k_a, rwkv_r_k,
              rwkv_ln_w, rwkv_ln_b, hgrn_lower_bounds, hgrn_norm_w, w_out, final_norm_w):
    lb_soft = jax.nn.softmax(hgrn_lower_bounds.astype(F32), axis=0)
    lb_all = jnp.cumsum(lb_soft, axis=0) - lb_soft[0]
    for l in range(DEPTH):
        h = _rmsnorm(x, norm_w[l])
        proj = (h @ w_in[l]).astype(F32)
        p_gdn, p_rwkv, p_hgrn = _split(proj, (GDN_PROJ, RWKV_PROJ, HGRN_PROJ))
        y_a = _gdn_branch(p_gdn, gdn_conv_w[l], gdn_a_log[l], gdn_dt_bias[l], gdn_norm_w[l])
        y_b = _rwkv7_branch(p_rwkv, rwkv_mu[l], rwkv_w0[l], rwkv_w_up[l], rwkv_a0[l], rwkv_a_up[l],
                            rwkv_k_k[l], rwkv_k_a[l], rwkv_r_k[l], rwkv_ln_w[l], rwkv_ln_b[l])
        y_c = _hgrn2_branch(p_hgrn, lb_all[l], hgrn_norm_w[l])
        y = jnp.concatenate([y_a, y_b, y_c], axis=-1).astype(x.dtype)
        x = x + y @ w_out[l]
    return _rmsnorm(x, final_norm_w)
```

```python
import functools

import jax
import jax.numpy as jnp
from jax import lax
from jax.experimental import pallas as pl
from jax.experimental.pallas import tpu as pltpu

F32 = jnp.float32
BF16 = jnp.bfloat16

HEAD_DIM = 64
HEADS = 8
WIDTH = HEADS * HEAD_DIM
CHUNK = 64
CONV_WIDTH = 4
LORA = 64
NORM_EPS = 1e-6
L2_EPS = 1e-6
RWKV_GN_EPS = 64e-5
LANES = 128
SUBLANES = 8
GDN_PROJ_PAD = 4 * WIDTH + LANES
RWKV_PROJ = 4 * WIDTH + 2 * LORA
HGRN_PROJ = 4 * WIDTH
TIME_BLOCK = 256
OUT_BLOCK = 512
VMEM_LIMIT = 56 * 1024 * 1024

HI = lax.Precision.HIGHEST


def _dot(a, b, precision=None):
    return jnp.dot(a, b, preferred_element_type=F32, precision=precision)


def _dot_nt(a, b, precision=None):
    return lax.dot_general(a, b, (((1,), (1,)), ((), ())), preferred_element_type=F32,
                           precision=precision)


def _dot_tn(a, b, precision=None):
    return lax.dot_general(a, b, (((0,), (0,)), ((), ())), preferred_element_type=F32,
                           precision=precision)


def _split_bf16(x, n):
    parts, r = [], x
    for _ in range(n):
        p = r.astype(BF16)
        parts.append(p)
        r = r - p.astype(F32)
    return parts


def _sel_dot(m01, x, n=3):
    out = None
    for p in _split_bf16(x, n):
        t = _dot(m01, p)
        out = t if out is None else out + t
    return out


def _dot_sel(x, m01, n=3):
    out = None
    for p in _split_bf16(x, n):
        t = _dot(p, m01)
        out = t if out is None else out + t
    return out


def _iota(shape, dim):
    return lax.broadcasted_iota(jnp.int32, shape, dim)


def _as01(mask):
    return jnp.where(mask, 1.0, 0.0).astype(BF16)


def _head_expand(offset=0):
    r = _iota((LANES, WIDTH), 0)
    c = _iota((LANES, WIDTH), 1)
    return _as01(r == c // HEAD_DIM + offset)


def _head_reduce():
    r = _iota((WIDTH, LANES), 0)
    c = _iota((WIDTH, LANES), 1)
    return _as01(c == r // HEAD_DIM)


def _chunk_cumsum_matrix(n):
    r = _iota((n, n), 0)
    c = _iota((n, n), 1)
    return _as01((r // CHUNK == c // CHUNK) & (c <= r))


def _tri_inv(b, eye):
    x = eye + b
    pw = b
    for _ in range(5):
        pw = _dot(pw, pw, HI)
        x = x + _dot(x, pw, HI)
    return x


def _silu(x):
    return x * jax.nn.sigmoid(x)


def _norm_proj(x_ref, nw_ref, w_ref):
    x = x_ref[...]
    ms = jnp.mean(x * x, axis=-1, keepdims=True)
    h = (x * lax.rsqrt(ms + NORM_EPS)) * nw_ref[...]
    return _dot(h.astype(BF16), w_ref[...])


def _tri_masks():
    r = _iota((CHUNK, CHUNK), 0)
    c = _iota((CHUNK, CHUNK), 1)
    return c <= r, c < r, jnp.where(r == c, 1.0, 0.0)


def _gdn_kernel(x_ref, nw_ref, w_ref, conv_ref, alog_ref, dtb_ref, gnw_ref, y_ref,
                s_ref, ext_ref, q_ref, k_ref, kb_ref, vb_ref, kbe_ref, qd_ref, gcf_ref, gcs_ref, o_ref):
    tb = x_ref.shape[0]

    @pl.when(pl.program_id(0) == 0)
    def _():
        s_ref[...] = jnp.zeros_like(s_ref)
        ext_ref[0:SUBLANES, :] = jnp.zeros((SUBLANES, 3 * WIDTH), F32)

    p = _norm_proj(x_ref, nw_ref, w_ref)
    gate = p[:, 3 * WIDTH:4 * WIDTH]
    small = p[:, 4 * WIDTH:4 * WIDTH + LANES]

    ext_ref[SUBLANES:SUBLANES + tb, :] = p[:, 0:3 * WIDTH]
    conv = None
    for j in range(CONV_WIDTH):
        t = conv_ref[j:j + 1, :] * ext_ref[pl.ds(SUBLANES - (CONV_WIDTH - 1) + j, tb), :]
        conv = t if conv is None else conv + t
    ext_ref[0:SUBLANES, :] = ext_ref[tb:tb + SUBLANES, :]
    qkv = _silu(conv)
    q, k, v = qkv[:, 0:WIDTH], qkv[:, WIDTH:2 * WIDTH], qkv[:, 2 * WIDTH:3 * WIDTH]

    red = _head_reduce()
    exp_b = _head_expand(0)
    exp_g = _head_expand(HEADS)
    q_rs = lax.rsqrt(_dot_sel(q * q, red) + L2_EPS)
    k_rs = lax.rsqrt(_dot_sel(k * k, red) + L2_EPS)
    qn = (q * _dot_sel(q_rs, exp_b)) * (HEAD_DIM ** -0.5)
    kn = k * _dot_sel(k_rs, exp_b)

    beta = _dot_sel(jax.nn.sigmoid(small), exp_b)
    g_small = -jnp.exp(alog_ref[...]) * jax.nn.softplus(small + dtb_ref[...])
    gc_small = _sel_dot(_chunk_cumsum_matrix(tb), g_small)
    gc_full = _dot_sel(gc_small, exp_g)
    egc = jnp.exp(gc_full)
    kb = kn * beta

    q_ref[...] = qn
    k_ref[...] = kn
    kb_ref[...] = kb
    vb_ref[...] = v * beta
    kbe_ref[...] = kb * egc
    qd_ref[...] = qn * egc
    gcf_ref[...] = gc_full
    gcs_ref[...] = gc_small

    causal, strict, eye = _tri_masks()

    def chunk_body(c, carry):
        r0 = pl.multiple_of(c * CHUNK, CHUNK)
        rows = pl.ds(r0, CHUNK)
        gc_c = gcf_ref[rows, :]
        gl = gcf_ref[pl.ds(r0 + CHUNK - 1, 1), :]
        kd_c = k_ref[rows, :] * jnp.exp(gl - gc_c)
        last = jnp.exp(gl)
        g_rows = gcs_ref[rows, :].T
        for h in range(HEADS):
            sl = slice(h * HEAD_DIM, (h + 1) * HEAD_DIM)
            diff = gc_c[:, sl] - g_rows[HEADS + h:HEADS + h + 1, :]
            dm = jnp.where(causal, jnp.exp(jnp.where(causal, diff, 0.0)), 0.0)
            k_h = k_ref[rows, sl]
            ak = _dot_nt(jnp.concatenate([kb_ref[rows, sl], q_ref[rows, sl]], axis=0), k_h)
            a = jnp.where(strict, ak[0:CHUNK] * dm, 0.0)
            qk = ak[CHUNK:2 * CHUNK] * dm
            tm = _tri_inv(-a, eye)
            uw = _dot(tm, jnp.concatenate([vb_ref[rows, sl], kbe_ref[rows, sl]], axis=1))
            s = s_ref[h]
            wq = _dot(jnp.concatenate([uw[:, HEAD_DIM:], qd_ref[rows, sl]], axis=0), s)
            v_new = uw[:, 0:HEAD_DIM] - wq[0:CHUNK]
            o_ref[rows, sl] = wq[CHUNK:2 * CHUNK] + _dot(qk, v_new)
            s_ref[h] = s * last[:, sl] + _dot_tn(kd_c[:, sl], v_new)
        return carry

    lax.fori_loop(0, tb // CHUNK, chunk_body, 0)

    o = o_ref[...]
    rs = lax.rsqrt(_dot_sel(o * o, red) * (1.0 / HEAD_DIM) + NORM_EPS)
    y_ref[...] = (o * _dot_sel(rs, exp_b)) * gnw_ref[...] * _silu(gate)


def _rwkv_kernel(x_ref, nw_ref, w_ref, mu_ref, w0_ref, wup_ref, a0_ref, aup_ref, kk_ref, ka_ref,
                 rk_ref, lnw_ref, lnb_ref, y_ref,
                 s_ref, ext_ref, qt_ref, rt_ref, ph_ref, kh_ref, pa_ref, k2_ref, v_ref, b_ref, o_ref):
    tb = x_ref.shape[0]

    @pl.when(pl.program_id(0) == 0)
    def _():
        s_ref[...] = jnp.zeros_like(s_ref)
        ext_ref[0:SUBLANES, :] = jnp.zeros((SUBLANES, RWKV_PROJ), F32)

    p = _norm_proj(x_ref, nw_ref, w_ref)
    ext_ref[SUBLANES:SUBLANES + tb, :] = p
    prev = ext_ref[pl.ds(SUBLANES - 1, tb), :]
    ext_ref[0:SUBLANES, :] = ext_ref[tb:tb + SUBLANES, :]
    p = p + mu_ref[...] * (prev - p)
    r, k, v = p[:, 0:WIDTH], p[:, WIDTH:2 * WIDTH], p[:, 2 * WIDTH:3 * WIDTH]
    wd = p[:, 3 * WIDTH:3 * WIDTH + LORA]
    ad = p[:, 3 * WIDTH + LORA:3 * WIDTH + 2 * LORA]
    gate = p[:, 3 * WIDTH + 2 * LORA:]

    w_raw = -jax.nn.softplus(-(w0_ref[...] + _dot(jnp.tanh(wd), wup_ref[...]))) - 0.5
    logw = -jnp.exp(w_raw)
    a = jax.nn.sigmoid(a0_ref[...] + _dot(ad, aup_ref[...]))

    red = _head_reduce()
    expd = _head_expand(0)
    kkp = k * kk_ref[...]
    kk = kkp * _dot_sel(lax.rsqrt(_dot_sel(kkp * kkp, red) + L2_EPS), expd)
    k2 = k * (1.0 + (a - 1.0) * ka_ref[...])
    pa = kk * a

    b = _sel_dot(_chunk_cumsum_matrix(tb), logw)
    enb = jnp.exp(-b)
    qt_ref[...] = -kk * jnp.exp(b - logw)
    rt_ref[...] = r * jnp.exp(b)
    ph_ref[...] = pa * enb
    kh_ref[...] = k2 * enb
    pa_ref[...] = pa
    k2_ref[...] = k2
    v_ref[...] = v
    b_ref[...] = b

    causal, strict, eye = _tri_masks()

    def chunk_body(c, carry):
        r0 = pl.multiple_of(c * CHUNK, CHUNK)
        rows = pl.ds(r0, CHUNK)
        bl = b_ref[pl.ds(r0 + CHUNK - 1, 1), :]
        ebl = jnp.exp(bl - b_ref[rows, :])
        pt_c = pa_ref[rows, :] * ebl
        kt_c = k2_ref[rows, :] * ebl
        wl = jnp.exp(bl)
        for h in range(HEADS):
            sl = slice(h * HEAD_DIM, (h + 1) * HEAD_DIM)
            qr = jnp.concatenate([qt_ref[rows, sl], rt_ref[rows, sl]], axis=0)
            pk = jnp.concatenate([ph_ref[rows, sl], kh_ref[rows, sl]], axis=0)
            aa = _dot_nt(qr, pk)
            a_qp = jnp.where(strict, aa[0:CHUNK, 0:CHUNK], 0.0)
            a_qk = jnp.where(strict, aa[0:CHUNK, CHUNK:], 0.0)
            a_rp = jnp.where(causal, aa[CHUNK:, 0:CHUNK], 0.0)
            a_rk = jnp.where(causal, aa[CHUNK:, CHUNK:], 0.0)
            tinv = _tri_inv(a_qp, eye)
            s = s_ref[h]
            qrs = _dot_nt(qr, s)
            v_h = v_ref[rows, sl]
            u = _dot(tinv, qrs[0:CHUNK] + _dot(a_qk, v_h))
            o_ref[rows, sl] = qrs[CHUNK:] + _dot(a_rp, u) + _dot(a_rk, v_h)
            s_ref[h] = s * wl[:, sl] + _dot_tn(u, pt_c[:, sl]) + _dot_tn(v_h, kt_c[:, sl])
        return carry

    lax.fori_loop(0, tb // CHUNK, chunk_body, 0)

    o = o_ref[...]
    inv_d = 1.0 / HEAD_DIM
    mean = _dot_sel(_dot_sel(o, red) * inv_d, expd)
    d = o - mean
    var = _dot_sel(d * d, red) * inv_d
    on = d * _dot_sel(lax.rsqrt(var + RWKV_GN_EPS), expd)
    on = on * lnw_ref[...] + lnb_ref[...]
    bonus = _dot_sel(_dot_sel(r * k2 * rk_ref[...], red), expd) * v
    y_ref[...] = (on + bonus) * _silu(gate)


def _level_matrix(n, s):
    r = _iota((n, n), 0)
    c = _iota((n, n), 1)
    tr, tc = r % CHUNK, c % CHUNK
    mid = (tr // (2 * s)) * (2 * s) + s - 1
    upper = (tr // s) % 2 == 1
    m = (upper & (tc > mid) & (tc <= tr)) | (jnp.logical_not(upper) & (tc > tr) & (tc <= mid))
    return _as01((r // CHUNK == c // CHUNK) & m)


def _hgrn_kernel(x_ref, nw_ref, w_ref, lbraw_ref, hnw_ref, y_ref,
                 s_ref, q_ref, kin_ref, v_ref, b_ref, qe_ref, ql_ref, kl_ref, o_ref, *, layer, depth):
    tb = x_ref.shape[0]
    levels = CHUNK.bit_length() - 1

    @pl.when(pl.program_id(0) == 0)
    def _():
        s_ref[...] = jnp.zeros_like(s_ref)

    lbs = [lbraw_ref[j:j + 1, :] for j in range(depth)]
    mx = lbs[0]
    for t in lbs[1:]:
        mx = jnp.maximum(mx, t)
    es = [jnp.exp(t - mx) for t in lbs]
    tot = es[0]
    for t in es[1:]:
        tot = tot + t
    soft = [t / tot for t in es]
    lb = soft[0]
    for t in soft[1:layer + 1]:
        lb = lb + t
    lb = lb - soft[0]

    p = _norm_proj(x_ref, nw_ref, w_ref)
    q = _silu(p[:, 0:WIDTH])
    f = p[:, WIDTH:2 * WIDTH]
    v = p[:, 2 * WIDTH:3 * WIDTH]
    gate = p[:, 3 * WIDTH:]
    log_g = jax.nn.log_sigmoid(f) + jnp.log1p(lb * jnp.exp(-f))
    kin = (1.0 - lb) * jax.nn.sigmoid(-f)

    b = _sel_dot(_chunk_cumsum_matrix(tb), log_g)
    q_ref[...] = q
    kin_ref[...] = kin
    v_ref[...] = v
    b_ref[...] = b
    qe_ref[...] = q * jnp.exp(b)
    tr = _iota((tb, WIDTH), 0) % CHUNK
    for l in range(levels):
        s = 1 << l
        e = jnp.exp(_sel_dot(_level_matrix(tb, s), log_g))
        upper = (tr // s) % 2 == 1
        ql_ref[l] = jnp.where(upper, q * e, 0.0)
        kl_ref[l] = jnp.where(upper, 0.0, kin * e)

    ri = _iota((CHUNK, CHUNK), 0)
    ci = _iota((CHUNK, CHUNK), 1)
    red = _head_reduce()
    expd = _head_expand(0)

    def chunk_body(c, carry):
        r0 = pl.multiple_of(c * CHUNK, CHUNK)
        rows = pl.ds(r0, CHUNK)
        bl = b_ref[pl.ds(r0 + CHUNK - 1, 1), :]
        kt_c = kin_ref[rows, :] * jnp.exp(bl - b_ref[rows, :])
        wl = jnp.exp(bl)
        for h in range(HEADS):
            sl = slice(h * HEAD_DIM, (h + 1) * HEAD_DIM)
            att = None
            for l in range(levels):
                s = 1 << l
                m = (ri // (2 * s) == ci // (2 * s)) & ((ri // s) % 2 == 1) & ((ci // s) % 2 == 0)
                t = jnp.where(m, _dot_nt(ql_ref[l, rows, sl], kl_ref[l, rows, sl]), 0.0)
                att = t if att is None else att + t
            v_h = v_ref[rows, sl]
            st = s_ref[h]
            o_ref[rows, sl] = _dot(att, v_h) + _dot_nt(qe_ref[rows, sl], st)
            s_ref[h] = st * wl[:, sl] + _dot_tn(v_h, kt_c[:, sl])
        return carry

    lax.fori_loop(0, tb // CHUNK, chunk_body, 0)

    o = o_ref[...] + _dot_sel(_dot_sel(q * kin, red), expd) * v
    ms = jnp.mean(o * o, axis=-1, keepdims=True)
    y_ref[...] = (o * lax.rsqrt(ms + NORM_EPS)) * hnw_ref[...] * _silu(gate)


def _outproj_kernel(x_ref, ya_ref, yb_ref, yc_ref, w_ref, fnw_ref, o_ref, *, final):
    acc = x_ref[...]
    for i, y_ref in enumerate((ya_ref, yb_ref, yc_ref)):
        acc = acc + _dot(y_ref[...].astype(BF16), w_ref[i * WIDTH:(i + 1) * WIDTH, :])
    if final:
        ms = jnp.mean(acc * acc, axis=-1, keepdims=True)
        acc = (acc * lax.rsqrt(ms + NORM_EPS)) * fnw_ref[...]
    o_ref[...] = acc


def _full(shape):
    nd = len(shape)
    return pl.BlockSpec(shape, lambda i, _nd=nd: (0,) * _nd)


def _rows(tb, width):
    return pl.BlockSpec((tb, width), lambda i: (i, 0))


def _mixer_call(body, name, x2, params, scratch):
    t, d = x2.shape
    tb = TIME_BLOCK
    return pl.pallas_call(
        body,
        grid=(t // tb,),
        in_specs=[_rows(tb, d)] + [_full(a.shape) for a in params],
        out_specs=_rows(tb, WIDTH),
        out_shape=jax.ShapeDtypeStruct((t, WIDTH), F32),
        scratch_shapes=scratch,
        compiler_params=pltpu.CompilerParams(dimension_semantics=("arbitrary",),
                                             vmem_limit_bytes=VMEM_LIMIT),
        name=name,
    )(x2, *params)


def _slab(tb):
    return pltpu.VMEM((tb, WIDTH), F32)


def _state():
    return pltpu.VMEM((HEADS, HEAD_DIM, HEAD_DIM), F32)


def _row(a):
    return a.reshape(1, -1).astype(F32)


def _lane_pad(a, offset):
    return jnp.zeros((1, LANES), F32).at[0, offset:offset + HEADS].set(a.astype(F32))


def kernel(x, norm_w, w_in, gdn_conv_w, gdn_a_log, gdn_dt_bias, gdn_norm_w, rwkv_mu, rwkv_w0, rwkv_w_up, rwkv_a0, rwkv_a_up, rwkv_k_k, rwkv_k_a, rwkv_r_k, rwkv_ln_w, rwkv_ln_b, hgrn_lower_bounds, hgrn_norm_w, w_out, final_norm_w):
    bsz, seq, d = x.shape
    depth = norm_w.shape[0]
    assert bsz == 1 and seq % TIME_BLOCK == 0 and seq % OUT_BLOCK == 0
    tb = TIME_BLOCK
    x2 = x.reshape(seq, d)

    gdn_cols = 3 * WIDTH + 2 * HEADS + WIDTH
    rwkv_off = gdn_cols
    hgrn_off = gdn_cols + RWKV_PROJ

    for l in range(depth):
        wl = w_in[l]
        nw = _row(norm_w[l])
        w_gdn = jnp.concatenate(
            [wl[:, 0:3 * WIDTH], wl[:, 3 * WIDTH + 2 * HEADS:gdn_cols],
             wl[:, 3 * WIDTH:3 * WIDTH + 2 * HEADS],
             jnp.zeros((d, LANES - 2 * HEADS), wl.dtype)], axis=1).astype(BF16)
        w_rwkv = wl[:, rwkv_off:rwkv_off + RWKV_PROJ].astype(BF16)
        w_hgrn = wl[:, hgrn_off:hgrn_off + HGRN_PROJ].astype(BF16)

        y_a = _mixer_call(
            _gdn_kernel, f"gdn_{l}", x2,
            [nw, w_gdn, gdn_conv_w[l].astype(F32), _lane_pad(gdn_a_log[l], HEADS),
             _lane_pad(gdn_dt_bias[l], HEADS), _row(jnp.tile(gdn_norm_w[l], HEADS))],
            [_state(), pltpu.VMEM((tb + SUBLANES, 3 * WIDTH), F32)] + [_slab(tb)] * 7
            + [pltpu.VMEM((tb, LANES), F32), _slab(tb)])

        y_b = _mixer_call(
            _rwkv_kernel, f"rwkv_{l}", x2,
            [nw, w_rwkv, _row(rwkv_mu[l]), _row(rwkv_w0[l]), rwkv_w_up[l].astype(F32),
             _row(rwkv_a0[l]), rwkv_a_up[l].astype(F32), _row(rwkv_k_k[l]), _row(rwkv_k_a[l]),
             _row(rwkv_r_k[l]), _row(rwkv_ln_w[l]), _row(rwkv_ln_b[l])],
            [_state(), pltpu.VMEM((tb + SUBLANES, RWKV_PROJ), F32)] + [_slab(tb)] * 9)

        levels = CHUNK.bit_length() - 1
        y_c = _mixer_call(
            functools.partial(_hgrn_kernel, layer=l, depth=depth), f"hgrn_{l}", x2,
            [nw, w_hgrn, hgrn_lower_bounds.astype(F32), _row(hgrn_norm_w[l])],
            [_state()] + [_slab(tb)] * 5
            + [pltpu.VMEM((levels, tb, WIDTH), F32)] * 2 + [_slab(tb)])

        final = l == depth - 1
        x2 = pl.pallas_call(
            functools.partial(_outproj_kernel, final=final),
            grid=(seq // OUT_BLOCK,),
            in_specs=[_rows(OUT_BLOCK, d)] + [_rows(OUT_BLOCK, WIDTH)] * 3
            + [_full((3 * WIDTH, d)), _full((1, d))],
            out_specs=_rows(OUT_BLOCK, d),
            out_shape=jax.ShapeDtypeStruct((seq, d), F32),
            compiler_params=pltpu.CompilerParams(dimension_semantics=("arbitrary",),
                                                 vmem_limit_bytes=VMEM_LIMIT),
            name=f"outproj_{l}",
        )(x2, y_a, y_b, y_c, w_out[l].astype(BF16), _row(final_norm_w))

    return x2.reshape(bsz, seq, d)
```

```python
import functools

import jax
import jax.numpy as jnp
from jax import lax
from jax.experimental import pallas as pl
from jax.experimental.pallas import tpu as pltpu

F32 = jnp.float32
BF16 = jnp.bfloat16

HEAD_DIM = 64
HEADS = 8
WIDTH = HEADS * HEAD_DIM
CHUNK = 64
CONV_WIDTH = 4
LORA = 64
NORM_EPS = 1e-6
L2_EPS = 1e-6
RWKV_GN_EPS = 64e-5
LANES = 128
SUBLANES = 8
GDN_PROJ_PAD = 4 * WIDTH + LANES
RWKV_PROJ = 4 * WIDTH + 2 * LORA
HGRN_PROJ = 4 * WIDTH
TIME_BLOCK = 256
OUT_BLOCK = 512
VMEM_LIMIT = 56 * 1024 * 1024

HI = lax.Precision.HIGHEST


def _dot(a, b, precision=None):
    return jnp.dot(a, b, preferred_element_type=F32, precision=precision)


def _dot_nt(a, b, precision=None):
    return lax.dot_general(a, b, (((1,), (1,)), ((), ())), preferred_element_type=F32,
                           precision=precision)


def _dot_tn(a, b, precision=None):
    return lax.dot_general(a, b, (((0,), (0,)), ((), ())), preferred_element_type=F32,
                           precision=precision)


def _split_bf16(x, n):
    parts, r = [], x
    for _ in range(n):
        p = r.astype(BF16)
        parts.append(p)
        r = r - p.astype(F32)
    return parts


def _sel_dot(m01, x, n=3):
    out = None
    for p in _split_bf16(x, n):
        t = _dot(m01, p)
        out = t if out is None else out + t
    return out


def _dot_sel(x, m01, n=3):
    out = None
    for p in _split_bf16(x, n):
        t = _dot(p, m01)
        out = t if out is None else out + t
    return out


def _iota(shape, dim):
    return lax.broadcasted_iota(jnp.int32, shape, dim)


def _as01(mask):
    return jnp.where(mask, 1.0, 0.0).astype(BF16)


def _head_expand(offset=0):
    r = _iota((LANES, WIDTH), 0)
    c = _iota((LANES, WIDTH), 1)
    return _as01(r == c // HEAD_DIM + offset)


def _head_reduce():
    r = _iota((WIDTH, LANES), 0)
    c = _iota((WIDTH, LANES), 1)
    return _as01(c == r // HEAD_DIM)


def _chunk_cumsum_matrix(n):
    r = _iota((n, n), 0)
    c = _iota((n, n), 1)
    return _as01((r // CHUNK == c // CHUNK) & (c <= r))


def _tri_inv_all(bs, eye):
    xs = [eye + b for b in bs]
    pws = list(bs)
    for _ in range(CHUNK.bit_length() - 2):
        pws = [_dot(pw, pw) for pw in pws]
        xs = [x + _dot(x, pw) for x, pw in zip(xs, pws)]
    return xs


def _silu(x):
    return x * jax.nn.sigmoid(x)


def _norm_proj(x_ref, nw_ref, w_ref):
    x = x_ref[...]
    ms = jnp.mean(x * x, axis=-1, keepdims=True)
    h = (x * lax.rsqrt(ms + NORM_EPS)) * nw_ref[...]
    return _dot(h.astype(BF16), w_ref[...])


def _tri_masks():
    r = _iota((CHUNK, CHUNK), 0)
    c = _iota((CHUNK, CHUNK), 1)
    return c <= r, c < r, jnp.where(r == c, 1.0, 0.0)


def _gdn_kernel(x_ref, nw_ref, w_ref, conv_ref, alog_ref, dtb_ref, gnw_ref, y_ref,
                s_ref, ext_ref, q_ref, k_ref, kb_ref, vb_ref, kbe_ref, qd_ref, gcf_ref, gcs_ref, o_ref):
    tb = x_ref.shape[0]

    @pl.when(pl.program_id(0) == 0)
    def _():
        s_ref[...] = jnp.zeros_like(s_ref)
        ext_ref[0:SUBLANES, :] = jnp.zeros((SUBLANES, 3 * WIDTH), F32)

    p = _norm_proj(x_ref, nw_ref, w_ref)
    gate = p[:, 3 * WIDTH:4 * WIDTH]
    small = p[:, 4 * WIDTH:4 * WIDTH + LANES]

    ext_ref[SUBLANES:SUBLANES + tb, :] = p[:, 0:3 * WIDTH]
    conv = None
    for j in range(CONV_WIDTH):
        t = conv_ref[j:j + 1, :] * ext_ref[pl.ds(SUBLANES - (CONV_WIDTH - 1) + j, tb), :]
        conv = t if conv is None else conv + t
    ext_ref[0:SUBLANES, :] = ext_ref[tb:tb + SUBLANES, :]
    qkv = _silu(conv)
    q, k, v = qkv[:, 0:WIDTH], qkv[:, WIDTH:2 * WIDTH], qkv[:, 2 * WIDTH:3 * WIDTH]

    red = _head_reduce()
    exp_b = _head_expand(0)
    exp_g = _head_expand(HEADS)
    q_rs = lax.rsqrt(_dot_sel(q * q, red) + L2_EPS)
    k_rs = lax.rsqrt(_dot_sel(k * k, red) + L2_EPS)
    qn = (q * _dot_sel(q_rs, exp_b)) * (HEAD_DIM ** -0.5)
    kn = k * _dot_sel(k_rs, exp_b)

    beta = _dot_sel(jax.nn.sigmoid(small), exp_b)
    g_small = -jnp.exp(alog_ref[...]) * jax.nn.softplus(small + dtb_ref[...])
    gc_small = _sel_dot(_chunk_cumsum_matrix(tb), g_small)
    gc_full = _dot_sel(gc_small, exp_g)
    egc = jnp.exp(gc_full)
    kb = kn * beta

    q_ref[...] = qn
    k_ref[...] = kn
    kb_ref[...] = kb
    vb_ref[...] = v * beta
    kbe_ref[...] = kb * egc
    qd_ref[...] = qn * egc
    gcf_ref[...] = gc_full
    gcs_ref[...] = gc_small

    causal, strict, eye = _tri_masks()

    def chunk_body(c, carry):
        r0 = pl.multiple_of(c * CHUNK, CHUNK)
        rows = pl.ds(r0, CHUNK)
        gc_c = gcf_ref[rows, :]
        gl = gcf_ref[pl.ds(r0 + CHUNK - 1, 1), :]
        kd_c = k_ref[rows, :] * jnp.exp(gl - gc_c)
        last = jnp.exp(gl)
        g_rows = gcs_ref[rows, :].T
        hs = range(HEADS)
        sls = [slice(h * HEAD_DIM, (h + 1) * HEAD_DIM) for h in hs]
        ak = [_dot_nt(jnp.concatenate([kb_ref[rows, sl], q_ref[rows, sl]], axis=0), k_ref[rows, sl])
              for sl in sls]
        dm = [jnp.where(causal, jnp.exp(jnp.where(
            causal, gc_c[:, sls[h]] - g_rows[HEADS + h:HEADS + h + 1, :], 0.0)), 0.0) for h in hs]
        a_neg = [jnp.where(strict, -(ak[h][0:CHUNK] * dm[h]), 0.0) for h in hs]
        qk = [ak[h][CHUNK:2 * CHUNK] * dm[h] for h in hs]
        tm = _tri_inv_all(a_neg, eye)
        uw = [_dot(tm[h], jnp.concatenate([vb_ref[rows, sls[h]], kbe_ref[rows, sls[h]]], axis=1))
              for h in hs]
        s = [s_ref[h] for h in hs]
        wq = [_dot(jnp.concatenate([uw[h][:, HEAD_DIM:], qd_ref[rows, sls[h]]], axis=0), s[h])
              for h in hs]
        v_new = [uw[h][:, 0:HEAD_DIM] - wq[h][0:CHUNK] for h in hs]
        ov = [_dot(qk[h], v_new[h]) for h in hs]
        sn = [_dot_tn(kd_c[:, sls[h]], v_new[h]) for h in hs]
        for h in hs:
            o_ref[rows, sls[h]] = wq[h][CHUNK:2 * CHUNK] + ov[h]
            s_ref[h] = s[h] * last[:, sls[h]] + sn[h]
        return carry

    lax.fori_loop(0, tb // CHUNK, chunk_body, 0)

    o = o_ref[...]
    rs = lax.rsqrt(_dot_sel(o * o, red) * (1.0 / HEAD_DIM) + NORM_EPS)
    y_ref[...] = (o * _dot_sel(rs, exp_b)) * gnw_ref[...] * _silu(gate)


def _rwkv_kernel(x_ref, nw_ref, w_ref, mu_ref, w0_ref, wup_ref, a0_ref, aup_ref, kk_ref, ka_ref,
                 rk_ref, lnw_ref, lnb_ref, y_ref,
                 s_ref, ext_ref, qt_ref, rt_ref, ph_ref, kh_ref, pa_ref, k2_ref, v_ref, b_ref, o_ref):
    tb = x_ref.shape[0]

    @pl.when(pl.program_id(0) == 0)
    def _():
        s_ref[...] = jnp.zeros_like(s_ref)
        ext_ref[0:SUBLANES, :] = jnp.zeros((SUBLANES, RWKV_PROJ), F32)

    p = _norm_proj(x_ref, nw_ref, w_ref)
    ext_ref[SUBLANES:SUBLANES + tb, :] = p
    prev = ext_ref[pl.ds(SUBLANES - 1, tb), :]
    ext_ref[0:SUBLANES, :] = ext_ref[tb:tb + SUBLANES, :]
    p = p + mu_ref[...] * (prev - p)
    r, k, v = p[:, 0:WIDTH], p[:, WIDTH:2 * WIDTH], p[:, 2 * WIDTH:3 * WIDTH]
    wd = p[:, 3 * WIDTH:3 * WIDTH + LORA]
    ad = p[:, 3 * WIDTH + LORA:3 * WIDTH + 2 * LORA]
    gate = p[:, 3 * WIDTH + 2 * LORA:]

    w_raw = -jax.nn.softplus(-(w0_ref[...] + _dot(jnp.tanh(wd), wup_ref[...]))) - 0.5
    logw = -jnp.exp(w_raw)
    a = jax.nn.sigmoid(a0_ref[...] + _dot(ad, aup_ref[...]))

    red = _head_reduce()
    expd = _head_expand(0)
    kkp = k * kk_ref[...]
    kk = kkp * _dot_sel(lax.rsqrt(_dot_sel(kkp * kkp, red) + L2_EPS), expd)
    k2 = k * (1.0 + (a - 1.0) * ka_ref[...])
    pa = kk * a

    b = _sel_dot(_chunk_cumsum_matrix(tb), logw)
    enb = jnp.exp(-b)
    qt_ref[...] = -kk * jnp.exp(b - logw)
    rt_ref[...] = r * jnp.exp(b)
    ph_ref[...] = pa * enb
    kh_ref[...] = k2 * enb
    pa_ref[...] = pa
    k2_ref[...] = k2
    v_ref[...] = v
    b_ref[...] = b

    causal, strict, eye = _tri_masks()

    def chunk_body(c, carry):
        r0 = pl.multiple_of(c * CHUNK, CHUNK)
        rows = pl.ds(r0, CHUNK)
        bl = b_ref[pl.ds(r0 + CHUNK - 1, 1), :]
        ebl = jnp.exp(bl - b_ref[rows, :])
        pt_c = pa_ref[rows, :] * ebl
        kt_c = k2_ref[rows, :] * ebl
        wl = jnp.exp(bl)
        hs = range(HEADS)
        sls = [slice(h * HEAD_DIM, (h + 1) * HEAD_DIM) for h in hs]
        qr = [jnp.concatenate([qt_ref[rows, sl], rt_ref[rows, sl]], axis=0) for sl in sls]
        pk = [jnp.concatenate([ph_ref[rows, sl], kh_ref[rows, sl]], axis=0) for sl in sls]
        aa = [_dot_nt(qr[h], pk[h]) for h in hs]
        a_qp = [jnp.where(strict, aa[h][0:CHUNK, 0:CHUNK], 0.0) for h in hs]
        tinv = _tri_inv_all(a_qp, eye)
        s = [s_ref[h] for h in hs]
        v_h = [v_ref[rows, sl] for sl in sls]
        qrs = [_dot_nt(qr[h], s[h]) for h in hs]
        akv = [_dot(jnp.where(strict, aa[h][0:CHUNK, CHUNK:], 0.0), v_h[h]) for h in hs]
        u = [_dot(tinv[h], qrs[h][0:CHUNK] + akv[h]) for h in hs]
        o1 = [_dot(jnp.where(causal, aa[h][CHUNK:, 0:CHUNK], 0.0), u[h]) for h in hs]
        o2 = [_dot(jnp.where(causal, aa[h][CHUNK:, CHUNK:], 0.0), v_h[h]) for h in hs]
        s1 = [_dot_tn(u[h], pt_c[:, sls[h]]) for h in hs]
        s2 = [_dot_tn(v_h[h], kt_c[:, sls[h]]) for h in hs]
        for h in hs:
            o_ref[rows, sls[h]] = qrs[h][CHUNK:] + o1[h] + o2[h]
            s_ref[h] = s[h] * wl[:, sls[h]] + s1[h] + s2[h]
        return carry

    lax.fori_loop(0, tb // CHUNK, chunk_body, 0)

    o = o_ref[...]
    inv_d = 1.0 / HEAD_DIM
    mean = _dot_sel(_dot_sel(o, red) * inv_d, expd)
    d = o - mean
    var = _dot_sel(d * d, red) * inv_d
    on = d * _dot_sel(lax.rsqrt(var + RWKV_GN_EPS), expd)
    on = on * lnw_ref[...] + lnb_ref[...]
    bonus = _dot_sel(_dot_sel(r * k2 * rk_ref[...], red), expd) * v
    y_ref[...] = (on + bonus) * _silu(gate)


def _level_matrix(n, s):
    r = _iota((n, n), 0)
    c = _iota((n, n), 1)
    tr, tc = r % CHUNK, c % CHUNK
    mid = (tr // (2 * s)) * (2 * s) + s - 1
    upper = (tr // s) % 2 == 1
    m = (upper & (tc > mid) & (tc <= tr)) | (jnp.logical_not(upper) & (tc > tr) & (tc <= mid))
    return _as01((r // CHUNK == c // CHUNK) & m)


def _hgrn_kernel(x_ref, nw_ref, w_ref, lbraw_ref, hnw_ref, y_ref,
                 s_ref, q_ref, kin_ref, v_ref, b_ref, qe_ref, ql_ref, kl_ref, o_ref, *, layer, depth):
    tb = x_ref.shape[0]
    levels = CHUNK.bit_length() - 1

    @pl.when(pl.program_id(0) == 0)
    def _():
        s_ref[...] = jnp.zeros_like(s_ref)

    lbs = [lbraw_ref[j:j + 1, :] for j in range(depth)]
    mx = lbs[0]
    for t in lbs[1:]:
        mx = jnp.maximum(mx, t)
    es = [jnp.exp(t - mx) for t in lbs]
    tot = es[0]
    for t in es[1:]:
        tot = tot + t
    soft = [t / tot for t in es]
    lb = soft[0]
    for t in soft[1:layer + 1]:
        lb = lb + t
    lb = lb - soft[0]

    p = _norm_proj(x_ref, nw_ref, w_ref)
    q = _silu(p[:, 0:WIDTH])
    f = p[:, WIDTH:2 * WIDTH]
    v = p[:, 2 * WIDTH:3 * WIDTH]
    gate = p[:, 3 * WIDTH:]
    log_g = jax.nn.log_sigmoid(f) + jnp.log1p(lb * jnp.exp(-f))
    kin = (1.0 - lb) * jax.nn.sigmoid(-f)

    b = _sel_dot(_chunk_cumsum_matrix(tb), log_g)
    q_ref[...] = q
    kin_ref[...] = kin
    v_ref[...] = v
    b_ref[...] = b
    qe_ref[...] = q * jnp.exp(b)
    tr = _iota((tb, WIDTH), 0) % CHUNK
    for l in range(levels):
        s = 1 << l
        e = jnp.exp(_sel_dot(_level_matrix(tb, s), log_g))
        upper = (tr // s) % 2 == 1
        ql_ref[l] = jnp.where(upper, q * e, 0.0)
        kl_ref[l] = jnp.where(upper, 0.0, kin * e)

    ri = _iota((CHUNK, CHUNK), 0)
    ci = _iota((CHUNK, CHUNK), 1)
    red = _head_reduce()
    expd = _head_expand(0)

    def chunk_body(c, carry):
        r0 = pl.multiple_of(c * CHUNK, CHUNK)
        rows = pl.ds(r0, CHUNK)
        bl = b_ref[pl.ds(r0 + CHUNK - 1, 1), :]
        kt_c = kin_ref[rows, :] * jnp.exp(bl - b_ref[rows, :])
        wl = jnp.exp(bl)
        for h in range(HEADS):
            sl = slice(h * HEAD_DIM, (h + 1) * HEAD_DIM)
            att = None
            for l in range(levels):
                s = 1 << l
                m = (ri // (2 * s) == ci // (2 * s)) & ((ri // s) % 2 == 1) & ((ci // s) % 2 == 0)
                t = jnp.where(m, _dot_nt(ql_ref[l, rows, sl], kl_ref[l, rows, sl]), 0.0)
                att = t if att is None else att + t
            v_h = v_ref[rows, sl]
            st = s_ref[h]
            o_ref[rows, sl] = _dot(att, v_h) + _dot_nt(qe_ref[rows, sl], st)
            s_ref[h] = st * wl[:, sl] + _dot_tn(v_h, kt_c[:, sl])
        return carry

    lax.fori_loop(0, tb // CHUNK, chunk_body, 0)

    o = o_ref[...] + _dot_sel(_dot_sel(q * kin, red), expd) * v
    ms = jnp.mean(o * o, axis=-1, keepdims=True)
    y_ref[...] = (o * lax.rsqrt(ms + NORM_EPS)) * hnw_ref[...] * _silu(gate)


def _outproj_kernel(x_ref, ya_ref, yb_ref, yc_ref, w_ref, fnw_ref, o_ref, *, final):
    acc = x_ref[...]
    for i, y_ref in enumerate((ya_ref, yb_ref, yc_ref)):
        acc = acc + _dot(y_ref[...].astype(BF16), w_ref[i * WIDTH:(i + 1) * WIDTH, :])
    if final:
        ms = jnp.mean(acc * acc, axis=-1, keepdims=True)
        acc = (acc * lax.rsqrt(ms + NORM_EPS)) * fnw_ref[...]
    o_ref[...] = acc


def _full(shape):
    nd = len(shape)
    return pl.BlockSpec(shape, lambda i, _nd=nd: (0,) * _nd)


def _rows(tb, width):
    return pl.BlockSpec((tb, width), lambda i: (i, 0))


def _mixer_call(body, name, x2, params, scratch):
    t, d = x2.shape
    tb = TIME_BLOCK
    return pl.pallas_call(
        body,
        grid=(t // tb,),
        in_specs=[_rows(tb, d)] + [_full(a.shape) for a in params],
        out_specs=_rows(tb, WIDTH),
        out_shape=jax.ShapeDtypeStruct((t, WIDTH), F32),
        scratch_shapes=scratch,
        compiler_params=pltpu.CompilerParams(dimension_semantics=("arbitrary",),
                                             vmem_limit_bytes=VMEM_LIMIT),
        name=name,
    )(x2, *params)


def _slab(tb):
    return pltpu.VMEM((tb, WIDTH), F32)


def _state():
    return pltpu.VMEM((HEADS, HEAD_DIM, HEAD_DIM), F32)


def _row(a):
    return a.reshape(1, -1).astype(F32)


def _lane_pad(a, offset):
    return jnp.zeros((1, LANES), F32).at[0, offset:offset + HEADS].set(a.astype(F32))


def kernel(x, norm_w, w_in, gdn_conv_w, gdn_a_log, gdn_dt_bias, gdn_norm_w, rwkv_mu, rwkv_w0, rwkv_w_up, rwkv_a0, rwkv_a_up, rwkv_k_k, rwkv_k_a, rwkv_r_k, rwkv_ln_w, rwkv_ln_b, hgrn_lower_bounds, hgrn_norm_w, w_out, final_norm_w):
    bsz, seq, d = x.shape
    depth = norm_w.shape[0]
    assert bsz == 1 and seq % TIME_BLOCK == 0 and seq % OUT_BLOCK == 0
    tb = TIME_BLOCK
    x2 = x.reshape(seq, d)

    gdn_cols = 3 * WIDTH + 2 * HEADS + WIDTH
    rwkv_off = gdn_cols
    hgrn_off = gdn_cols + RWKV_PROJ

    for l in range(depth):
        wl = w_in[l]
        nw = _row(norm_w[l])
        w_gdn = jnp.concatenate(
            [wl[:, 0:3 * WIDTH], wl[:, 3 * WIDTH + 2 * HEADS:gdn_cols],
             wl[:, 3 * WIDTH:3 * WIDTH + 2 * HEADS],
             jnp.zeros((d, LANES - 2 * HEADS), wl.dtype)], axis=1).astype(BF16)
        w_rwkv = wl[:, rwkv_off:rwkv_off + RWKV_PROJ].astype(BF16)
        w_hgrn = wl[:, hgrn_off:hgrn_off + HGRN_PROJ].astype(BF16)

        y_a = _mixer_call(
            _gdn_kernel, f"gdn_{l}", x2,
            [nw, w_gdn, gdn_conv_w[l].astype(F32), _lane_pad(gdn_a_log[l], HEADS),
             _lane_pad(gdn_dt_bias[l], HEADS), _row(jnp.tile(gdn_norm_w[l], HEADS))],
            [_state(), pltpu.VMEM((tb + SUBLANES, 3 * WIDTH), F32)] + [_slab(tb)] * 7
            + [pltpu.VMEM((tb, LANES), F32), _slab(tb)])

        y_b = _mixer_call(
            _rwkv_kernel, f"rwkv_{l}", x2,
            [nw, w_rwkv, _row(rwkv_mu[l]), _row(rwkv_w0[l]), rwkv_w_up[l].astype(F32),
             _row(rwkv_a0[l]), rwkv_a_up[l].astype(F32), _row(rwkv_k_k[l]), _row(rwkv_k_a[l]),
             _row(rwkv_r_k[l]), _row(rwkv_ln_w[l]), _row(rwkv_ln_b[l])],
            [_state(), pltpu.VMEM((tb + SUBLANES, RWKV_PROJ), F32)] + [_slab(tb)] * 9)

        levels = CHUNK.bit_length() - 1
        y_c = _mixer_call(
            functools.partial(_hgrn_kernel, layer=l, depth=depth), f"hgrn_{l}", x2,
            [nw, w_hgrn, hgrn_lower_bounds.astype(F32), _row(hgrn_norm_w[l])],
            [_state()] + [_slab(tb)] * 5
            + [pltpu.VMEM((levels, tb, WIDTH), F32)] * 2 + [_slab(tb)])

        final = l == depth - 1
        x2 = pl.pallas_call(
            functools.partial(_outproj_kernel, final=final),
            grid=(seq // OUT_BLOCK,),
            in_specs=[_rows(OUT_BLOCK, d)] + [_rows(OUT_BLOCK, WIDTH)] * 3
            + [_full((3 * WIDTH, d)), _full((1, d))],
            out_specs=_rows(OUT_BLOCK, d),
            out_shape=jax.ShapeDtypeStruct((seq, d), F32),
            compiler_params=pltpu.CompilerParams(dimension_semantics=("arbitrary",),
                                                 vmem_limit_bytes=VMEM_LIMIT),
            name=f"outproj_{l}",
        )(x2, y_a, y_b, y_c, w_out[l].astype(BF16), _row(final_norm_w))

    return x2.reshape(bsz, seq, d)
```

```python
import functools

import numpy as np
import jax
import jax.numpy as jnp
from jax import lax
from jax.experimental import pallas as pl
from jax.experimental.pallas import tpu as pltpu

F32 = jnp.float32
BF16 = jnp.bfloat16

HEAD_DIM = 64
HEADS = 8
WIDTH = HEADS * HEAD_DIM
CHUNK = 64
LEVELS = CHUNK.bit_length() - 1
CONV_WIDTH = 4
LORA = 64
NORM_EPS = 1e-6
L2_EPS = 1e-6
RWKV_GN_EPS = 64e-5
LANES = 128
SUBLANES = 8
RWKV_PROJ = 4 * WIDTH + 2 * LORA
HGRN_PROJ = 4 * WIDTH
TIME_BLOCK = 256
PREP_CHUNKS = 2
OUT_BLOCK = 512
VMEM_LIMIT = 56 * 1024 * 1024


def _dot(a, b):
    return jnp.dot(a, b, preferred_element_type=F32)


def _dot_nt(a, b):
    return lax.dot_general(a, b, (((1,), (1,)), ((), ())), preferred_element_type=F32)


def _dot_tn(a, b):
    return lax.dot_general(a, b, (((0,), (0,)), ((), ())), preferred_element_type=F32)


def _split_bf16(x, n):
    parts, r = [], x
    for i in range(n):
        p = r.astype(BF16)
        parts.append(p)
        if i + 1 < n:
            r = r - p.astype(F32)
    return parts


def _sel_dot(m01, x, n):
    out = None
    for p in _split_bf16(x, n):
        t = _dot(m01, p)
        out = t if out is None else out + t
    return out


def _dot_sel(x, m01, n):
    out = None
    for p in _split_bf16(x, n):
        t = _dot(p, m01)
        out = t if out is None else out + t
    return out


def _iota(shape, dim):
    return lax.broadcasted_iota(jnp.int32, shape, dim)


def _tri_inv_all(bs, eye):
    xs = [eye + b for b in bs]
    pws = list(bs)
    for _ in range(LEVELS - 1):
        pws = [_dot(pw, pw) for pw in pws]
        xs = [x + _dot(x, pw) for x, pw in zip(xs, pws)]
    return xs


def _tri_solve_all(bs, ys):
    ps = list(bs)
    for j in range(LEVELS):
        if j + 1 < LEVELS:
            r = [_dot(p, jnp.concatenate([y, p], axis=1)) for p, y in zip(ps, ys)]
            ps = [t[:, 2 * HEAD_DIM:] for t in r]
            ys = [y + t[:, 0:2 * HEAD_DIM] for y, t in zip(ys, r)]
        else:
            ys = [y + _dot(p, y) for p, y in zip(ps, ys)]
    return ys


def _silu(x):
    return x * jax.nn.sigmoid(x)


def _norm_proj(x_ref, nw_ref, w_ref):
    x = x_ref[...]
    ms = jnp.mean(x * x, axis=-1, keepdims=True)
    h = (x * lax.rsqrt(ms + NORM_EPS)) * nw_ref[...]
    return _dot(h.astype(BF16), w_ref[...])


def _tri_masks():
    r = _iota((CHUNK, CHUNK), 0)
    c = _iota((CHUNK, CHUNK), 1)
    return c <= r, c < r, jnp.where(r == c, 1.0, 0.0)


def _head_slices():
    return [slice(h * HEAD_DIM, (h + 1) * HEAD_DIM) for h in range(HEADS)]


def _np_head_expand(offset):
    r = np.arange(LANES)[:, None]
    c = np.arange(WIDTH)[None, :]
    return r == c // HEAD_DIM + offset


def _np_head_reduce():
    return _np_head_expand(0).T


def _np_chunk_cumsum(n):
    r = np.arange(n)[:, None]
    c = np.arange(n)[None, :]
    return (r // CHUNK == c // CHUNK) & (c <= r)


def _np_level_matrix(n, s):
    r = np.arange(n)[:, None]
    c = np.arange(n)[None, :]
    tr, tc = r % CHUNK, c % CHUNK
    mid = (tr // (2 * s)) * (2 * s) + s - 1
    upper = (tr // s) % 2 == 1
    m = (upper & (tc > mid) & (tc <= tr)) | (~upper & (tc > tr) & (tc <= mid))
    return (r // CHUNK == c // CHUNK) & m


def _const(mask):
    return jnp.asarray(np.asarray(mask, np.float32), BF16)


def _gdn_kernel(x_ref, nw_ref, w_ref, conv_ref, alog_ref, dtb_ref, gnw_ref, cum_ref, red_ref, expb_ref,
                expg_ref, y_ref,
                s_ref, ext_ref, q_ref, k_ref, kb_ref, vb_ref, kbe_ref, qd_ref, gcf_ref, gcs_ref,
                qm_ref, on_ref, o_ref):
    tb = x_ref.shape[0]

    @pl.when(pl.program_id(0) == 0)
    def _():
        s_ref[...] = jnp.zeros_like(s_ref)
        ext_ref[0:SUBLANES, :] = jnp.zeros((SUBLANES, 3 * WIDTH), F32)

    p = _norm_proj(x_ref, nw_ref, w_ref)
    gate = p[:, 3 * WIDTH:4 * WIDTH]
    small = p[:, 4 * WIDTH:4 * WIDTH + LANES]

    ext_ref[SUBLANES:SUBLANES + tb, :] = p[:, 0:3 * WIDTH]
    conv = None
    for j in range(CONV_WIDTH):
        t = conv_ref[j:j + 1, :] * ext_ref[pl.ds(SUBLANES - (CONV_WIDTH - 1) + j, tb), :]
        conv = t if conv is None else conv + t
    ext_ref[0:SUBLANES, :] = ext_ref[tb:tb + SUBLANES, :]
    qkv = _silu(conv)
    q, k, v = qkv[:, 0:WIDTH], qkv[:, WIDTH:2 * WIDTH], qkv[:, 2 * WIDTH:3 * WIDTH]

    red = red_ref[...]
    exp_b = expb_ref[...]
    q_rs = lax.rsqrt(_dot_sel(q * q, red, 1) + L2_EPS)
    k_rs = lax.rsqrt(_dot_sel(k * k, red, 1) + L2_EPS)
    qn = (q * _dot_sel(q_rs, exp_b, 2)) * (HEAD_DIM ** -0.5)
    kn = k * _dot_sel(k_rs, exp_b, 2)

    beta = _dot_sel(jax.nn.sigmoid(small), exp_b, 2)
    g_small = -jnp.exp(alog_ref[...]) * jax.nn.softplus(small + dtb_ref[...])
    gc_small = _sel_dot(cum_ref[...], g_small, 3)
    gc_full = _dot_sel(gc_small, expg_ref[...], 3)
    egc = jnp.exp(gc_full)
    kb = kn * beta

    q_ref[...] = qn
    k_ref[...] = kn
    kb_ref[...] = kb
    vb_ref[...] = v * beta
    kbe_ref[...] = kb * egc
    qd_ref[...] = qn * egc
    gcf_ref[...] = gc_full
    gcs_ref[...] = gc_small

    causal, strict, eye = _tri_masks()
    sls = _head_slices()

    def prep_body(i, carry):
        cs = [i * PREP_CHUNKS + j for j in range(PREP_CHUNKS)]
        rws = [pl.ds(pl.multiple_of(c * CHUNK, CHUNK), CHUNK) for c in cs]
        gc_c = [gcf_ref[rows, :] for rows in rws]
        gl = [gcf_ref[pl.ds(pl.multiple_of(c * CHUNK, CHUNK) + CHUNK - 1, 1), :] for c in cs]
        kd_c = [k_ref[rows, :] * jnp.exp(g1 - g2) for rows, g1, g2 in zip(rws, gl, gc_c)]
        last = [jnp.exp(g1) for g1 in gl]
        g_rows = [gcs_ref[rows, :].T for rows in rws]
        items = [(j, h) for j in range(PREP_CHUNKS) for h in range(HEADS)]
        ak = [_dot_nt(jnp.concatenate([kb_ref[rws[j], sls[h]], q_ref[rws[j], sls[h]]], axis=0),
                      k_ref[rws[j], sls[h]]) for j, h in items]
        dm = [jnp.where(causal, jnp.exp(jnp.where(
            causal, gc_c[j][:, sls[h]] - g_rows[j][HEADS + h:HEADS + h + 1, :], 0.0)), 0.0)
            for j, h in items]
        a_neg = [jnp.where(strict, -(t[0:CHUNK] * d), 0.0) for t, d in zip(ak, dm)]
        qk = [t[CHUNK:2 * CHUNK] * d for t, d in zip(ak, dm)]
        uw = _tri_solve_all(a_neg, [jnp.concatenate([vb_ref[rws[j], sls[h]], kbe_ref[rws[j], sls[h]]],
                                                    axis=1) for j, h in items])
        t1 = [_dot(a, b) for a, b in zip(qk, uw)]
        t2 = [_dot_tn(kd_c[j][:, sls[h]], b) for (j, h), b in zip(items, uw)]
        for n, (j, h) in enumerate(items):
            qm_ref[cs[j], h, 0:CHUNK, :] = qd_ref[rws[j], sls[h]] - t1[n][:, HEAD_DIM:]
            qm_ref[cs[j], h, CHUNK:, :] = eye * last[j][:, sls[h]] - t2[n][:, HEAD_DIM:]
            on_ref[cs[j], h, 0:CHUNK, :] = t1[n][:, 0:HEAD_DIM]
            on_ref[cs[j], h, CHUNK:, :] = t2[n][:, 0:HEAD_DIM]
        return carry

    lax.fori_loop(0, tb // (CHUNK * PREP_CHUNKS), prep_body, 0)

    def scan_body(c, carry):
        rows = pl.ds(pl.multiple_of(c * CHUNK, CHUNK), CHUNK)
        out = [_dot(qm_ref[c, h], s_ref[h]) + on_ref[c, h] for h in range(HEADS)]
        for h in range(HEADS):
            s_ref[h] = out[h][CHUNK:]
            o_ref[rows, sls[h]] = out[h][0:CHUNK]
        return carry

    lax.fori_loop(0, tb // CHUNK, scan_body, 0)

    o = o_ref[...]
    rs = lax.rsqrt(_dot_sel(o * o, red, 1) * (1.0 / HEAD_DIM) + NORM_EPS)
    y_ref[...] = (o * _dot_sel(rs, exp_b, 2)) * gnw_ref[...] * _silu(gate)


def _rwkv_kernel(x_ref, nw_ref, w_ref, mu_ref, w0_ref, wup_ref, a0_ref, aup_ref, kk_ref, ka_ref,
                 rk_ref, lnw_ref, lnb_ref, cum_ref, red_ref, expd_ref, y_ref,
                 s_ref, ext_ref, qt_ref, rt_ref, ph_ref, kh_ref, pa_ref, k2_ref, v_ref, b_ref, o_ref):
    tb = x_ref.shape[0]

    @pl.when(pl.program_id(0) == 0)
    def _():
        s_ref[...] = jnp.zeros_like(s_ref)
        ext_ref[0:SUBLANES, :] = jnp.zeros((SUBLANES, RWKV_PROJ), F32)

    p = _norm_proj(x_ref, nw_ref, w_ref)
    ext_ref[SUBLANES:SUBLANES + tb, :] = p
    prev = ext_ref[pl.ds(SUBLANES - 1, tb), :]
    ext_ref[0:SUBLANES, :] = ext_ref[tb:tb + SUBLANES, :]
    p = p + mu_ref[...] * (prev - p)
    r, k, v = p[:, 0:WIDTH], p[:, WIDTH:2 * WIDTH], p[:, 2 * WIDTH:3 * WIDTH]
    wd = p[:, 3 * WIDTH:3 * WIDTH + LORA]
    ad = p[:, 3 * WIDTH + LORA:3 * WIDTH + 2 * LORA]
    gate = p[:, 3 * WIDTH + 2 * LORA:]

    w_raw = -jax.nn.softplus(-(w0_ref[...] + _dot(jnp.tanh(wd), wup_ref[...]))) - 0.5
    logw = -jnp.exp(w_raw)
    a = jax.nn.sigmoid(a0_ref[...] + _dot(ad, aup_ref[...]))

    red = red_ref[...]
    expd = expd_ref[...]
    kkp = k * kk_ref[...]
    kk = kkp * _dot_sel(lax.rsqrt(_dot_sel(kkp * kkp, red, 1) + L2_EPS), expd, 2)
    k2 = k * (1.0 + (a - 1.0) * ka_ref[...])
    pa = kk * a

    b = _sel_dot(cum_ref[...], logw, 2)
    enb = jnp.exp(-b)
    qt_ref[...] = -kk * jnp.exp(b - logw)
    rt_ref[...] = r * jnp.exp(b)
    ph_ref[...] = pa * enb
    kh_ref[...] = k2 * enb
    pa_ref[...] = pa
    k2_ref[...] = k2
    v_ref[...] = v
    b_ref[...] = b

    causal, strict, eye = _tri_masks()
    sls = _head_slices()

    def chunk_body(c, carry):
        r0 = pl.multiple_of(c * CHUNK, CHUNK)
        rows = pl.ds(r0, CHUNK)
        bl = b_ref[pl.ds(r0 + CHUNK - 1, 1), :]
        ebl = jnp.exp(bl - b_ref[rows, :])
        pt_c = pa_ref[rows, :] * ebl
        kt_c = k2_ref[rows, :] * ebl
        wl = jnp.exp(bl)
        hs = range(HEADS)
        qr = [jnp.concatenate([qt_ref[rows, sl], rt_ref[rows, sl]], axis=0) for sl in sls]
        pk = [jnp.concatenate([ph_ref[rows, sl], kh_ref[rows, sl]], axis=0) for sl in sls]
        aa = [_dot_nt(qr[h], pk[h]) for h in hs]
        tinv = _tri_inv_all([jnp.where(strict, aa[h][0:CHUNK, 0:CHUNK], 0.0) for h in hs], eye)
        s = [s_ref[h] for h in hs]
        v_h = [v_ref[rows, sl] for sl in sls]
        qrs = [_dot_nt(qr[h], s[h]) for h in hs]
        akv = [_dot(jnp.where(strict, aa[h][0:CHUNK, CHUNK:], 0.0), v_h[h]) for h in hs]
        u = [_dot(tinv[h], qrs[h][0:CHUNK] + akv[h]) for h in hs]
        o1 = [_dot(jnp.where(causal, aa[h][CHUNK:, 0:CHUNK], 0.0), u[h]) for h in hs]
        o2 = [_dot(jnp.where(causal, aa[h][CHUNK:, CHUNK:], 0.0), v_h[h]) for h in hs]
        s1 = [_dot_tn(u[h], pt_c[:, sls[h]]) for h in hs]
        s2 = [_dot_tn(v_h[h], kt_c[:, sls[h]]) for h in hs]
        for h in hs:
            o_ref[rows, sls[h]] = qrs[h][CHUNK:] + o1[h] + o2[h]
            s_ref[h] = s[h] * wl[:, sls[h]] + s1[h] + s2[h]
        return carry

    lax.fori_loop(0, tb // CHUNK, chunk_body, 0)

    o = o_ref[...]
    inv_d = 1.0 / HEAD_DIM
    mean = _dot_sel(_dot_sel(o, red, 2) * inv_d, expd, 2)
    d = o - mean
    var = _dot_sel(d * d, red, 1) * inv_d
    on = d * _dot_sel(lax.rsqrt(var + RWKV_GN_EPS), expd, 2)
    on = on * lnw_ref[...] + lnb_ref[...]
    bonus = _dot_sel(_dot_sel(r * k2 * rk_ref[...], red, 2), expd, 2) * v
    y_ref[...] = (on + bonus) * _silu(gate)


def _hgrn_kernel(x_ref, nw_ref, w_ref, lbraw_ref, hnw_ref, cum_ref, lvl_ref, red_ref, expd_ref, y_ref,
                 s_ref, kin_ref, v_ref, b_ref, qe_ref, ql_ref, kl_ref, o_ref, *, layer, depth):
    tb = x_ref.shape[0]

    @pl.when(pl.program_id(0) == 0)
    def _():
        s_ref[...] = jnp.zeros_like(s_ref)

    lbs = [lbraw_ref[j:j + 1, :] for j in range(depth)]
    mx = lbs[0]
    for t in lbs[1:]:
        mx = jnp.maximum(mx, t)
    es = [jnp.exp(t - mx) for t in lbs]
    tot = es[0]
    for t in es[1:]:
        tot = tot + t
    soft = [t / tot for t in es]
    lb = soft[0]
    for t in soft[1:layer + 1]:
        lb = lb + t
    lb = lb - soft[0]

    p = _norm_proj(x_ref, nw_ref, w_ref)
    q = _silu(p[:, 0:WIDTH])
    f = p[:, WIDTH:2 * WIDTH]
    v = p[:, 2 * WIDTH:3 * WIDTH]
    gate = p[:, 3 * WIDTH:]
    log_g = jax.nn.log_sigmoid(f) + jnp.log1p(lb * jnp.exp(-f))
    kin = (1.0 - lb) * jax.nn.sigmoid(-f)

    b = _sel_dot(cum_ref[...], log_g, 2)
    kin_ref[...] = kin
    v_ref[...] = v
    b_ref[...] = b
    qe_ref[...] = q * jnp.exp(b)
    lg_parts = _split_bf16(log_g, 2)
    tr = _iota((tb, WIDTH), 0)
    for l in range(LEVELS):
        e = jnp.exp(_dot(lvl_ref[l], lg_parts[0]) + _dot(lvl_ref[l], lg_parts[1]))
        upper = (tr & (1 << l)) != 0
        ql_ref[l] = jnp.where(upper, q * e, 0.0)
        kl_ref[l] = jnp.where(upper, 0.0, kin * e)

    ri = _iota((CHUNK, CHUNK), 0)
    ci = _iota((CHUNK, CHUNK), 1)
    lvl_masks = [((ri >> (l + 1)) == (ci >> (l + 1))) & ((ri & (1 << l)) != 0) & ((ci & (1 << l)) == 0)
                 for l in range(LEVELS)]
    hs = range(HEADS)
    sls = _head_slices()

    def chunk_body(c, carry):
        r0 = pl.multiple_of(c * CHUNK, CHUNK)
        rows = pl.ds(r0, CHUNK)
        bl = b_ref[pl.ds(r0 + CHUNK - 1, 1), :]
        kt_c = kin_ref[rows, :] * jnp.exp(bl - b_ref[rows, :])
        wl = jnp.exp(bl)
        att = [None] * HEADS
        for l in range(LEVELS):
            for h in hs:
                t = jnp.where(lvl_masks[l], _dot_nt(ql_ref[l, rows, sls[h]], kl_ref[l, rows, sls[h]]), 0.0)
                att[h] = t if att[h] is None else att[h] + t
        v_h = [v_ref[rows, sl] for sl in sls]
        st = [s_ref[h] for h in hs]
        oi = [_dot(att[h], v_h[h]) for h in hs]
        oc = [_dot_nt(qe_ref[rows, sls[h]], st[h]) for h in hs]
        sn = [_dot_tn(v_h[h], kt_c[:, sls[h]]) for h in hs]
        for h in hs:
            o_ref[rows, sls[h]] = oi[h] + oc[h]
            s_ref[h] = st[h] * wl[:, sls[h]] + sn[h]
        return carry

    lax.fori_loop(0, tb // CHUNK, chunk_body, 0)

    o = o_ref[...] + _dot_sel(_dot_sel(q * kin, red_ref[...], 2), expd_ref[...], 2) * v
    ms = jnp.mean(o * o, axis=-1, keepdims=True)
    y_ref[...] = (o * lax.rsqrt(ms + NORM_EPS)) * hnw_ref[...] * _silu(gate)


def _outproj_kernel(x_ref, ya_ref, yb_ref, yc_ref, w_ref, fnw_ref, o_ref, *, final):
    acc = x_ref[...]
    for i, y_ref in enumerate((ya_ref, yb_ref, yc_ref)):
        acc = acc + _dot(y_ref[...].astype(BF16), w_ref[i * WIDTH:(i + 1) * WIDTH, :])
    if final:
        ms = jnp.mean(acc * acc, axis=-1, keepdims=True)
        acc = (acc * lax.rsqrt(ms + NORM_EPS)) * fnw_ref[...]
    o_ref[...] = acc


def _full(shape):
    nd = len(shape)
    return pl.BlockSpec(shape, lambda i, _nd=nd: (0,) * _nd)


def _rows(tb, width):
    return pl.BlockSpec((tb, width), lambda i: (i, 0))


def _mixer_call(body, name, x2, params, scratch):
    t, d = x2.shape
    tb = TIME_BLOCK
    return pl.pallas_call(
        body,
        grid=(t // tb,),
        in_specs=[_rows(tb, d)] + [_full(a.shape) for a in params],
        out_specs=_rows(tb, WIDTH),
        out_shape=jax.ShapeDtypeStruct((t, WIDTH), F32),
        scratch_shapes=scratch,
        compiler_params=pltpu.CompilerParams(dimension_semantics=("arbitrary",),
                                             vmem_limit_bytes=VMEM_LIMIT),
        name=name,
    )(x2, *params)


def _slab(tb):
    return pltpu.VMEM((tb, WIDTH), F32)


def _state():
    return pltpu.VMEM((HEADS, HEAD_DIM, HEAD_DIM), F32)


def _folded(nc):
    return pltpu.VMEM((nc, HEADS, 2 * CHUNK, HEAD_DIM), F32)


def _row(a):
    return a.reshape(1, -1).astype(F32)


def _lane_pad(a, offset):
    return jnp.zeros((1, LANES), F32).at[0, offset:offset + HEADS].set(a.astype(F32))


def kernel(x, norm_w, w_in, gdn_conv_w, gdn_a_log, gdn_dt_bias, gdn_norm_w, rwkv_mu, rwkv_w0, rwkv_w_up, rwkv_a0, rwkv_a_up, rwkv_k_k, rwkv_k_a, rwkv_r_k, rwkv_ln_w, rwkv_ln_b, hgrn_lower_bounds, hgrn_norm_w, w_out, final_norm_w):
    bsz, seq, d = x.shape
    depth = norm_w.shape[0]
    assert bsz == 1 and seq % TIME_BLOCK == 0 and seq % OUT_BLOCK == 0
    tb = TIME_BLOCK
    nc = tb // CHUNK
    x2 = x.reshape(seq, d)

    gdn_cols = 3 * WIDTH + 2 * HEADS + WIDTH
    rwkv_off = gdn_cols
    hgrn_off = gdn_cols + RWKV_PROJ

    cum = _const(_np_chunk_cumsum(tb))
    red = _const(_np_head_reduce())
    exp0 = _const(_np_head_expand(0))
    exp8 = _const(_np_head_expand(HEADS))
    lvl = _const(np.stack([_np_level_matrix(tb, 1 << l) for l in range(LEVELS)]))

    for l in range(depth):
        wl = w_in[l]
        nw = _row(norm_w[l])
        w_gdn = jnp.concatenate(
            [wl[:, 0:3 * WIDTH], wl[:, 3 * WIDTH + 2 * HEADS:gdn_cols],
             wl[:, 3 * WIDTH:3 * WIDTH + 2 * HEADS],
             jnp.zeros((d, LANES - 2 * HEADS), wl.dtype)], axis=1).astype(BF16)
        w_rwkv = wl[:, rwkv_off:rwkv_off + RWKV_PROJ].astype(BF16)
        w_hgrn = wl[:, hgrn_off:hgrn_off + HGRN_PROJ].astype(BF16)

        y_a = _mixer_call(
            _gdn_kernel, f"gdn_{l}", x2,
            [nw, w_gdn, gdn_conv_w[l].astype(F32), _lane_pad(gdn_a_log[l], HEADS),
             _lane_pad(gdn_dt_bias[l], HEADS), _row(jnp.tile(gdn_norm_w[l], HEADS)),
             cum, red, exp0, exp8],
            [_state(), pltpu.VMEM((tb + SUBLANES, 3 * WIDTH), F32)] + [_slab(tb)] * 7
            + [pltpu.VMEM((tb, LANES), F32), _folded(nc), _folded(nc), _slab(tb)])

        y_b = _mixer_call(
            _rwkv_kernel, f"rwkv_{l}", x2,
            [nw, w_rwkv, _row(rwkv_mu[l]), _row(rwkv_w0[l]), rwkv_w_up[l].astype(F32),
             _row(rwkv_a0[l]), rwkv_a_up[l].astype(F32), _row(rwkv_k_k[l]), _row(rwkv_k_a[l]),
             _row(rwkv_r_k[l]), _row(rwkv_ln_w[l]), _row(rwkv_ln_b[l]), cum, red, exp0],
            [_state(), pltpu.VMEM((tb + SUBLANES, RWKV_PROJ), F32)] + [_slab(tb)] * 9)

        y_c = _mixer_call(
            functools.partial(_hgrn_kernel, layer=l, depth=depth), f"hgrn_{l}", x2,
            [nw, w_hgrn, hgrn_lower_bounds.astype(F32), _row(hgrn_norm_w[l]), cum, lvl, red, exp0],
            [_state()] + [_slab(tb)] * 4
            + [pltpu.VMEM((LEVELS, tb, WIDTH), F32)] * 2 + [_slab(tb)])

        final = l == depth - 1
        x2 = pl.pallas_call(
            functools.partial(_outproj_kernel, final=final),
            grid=(seq // OUT_BLOCK,),
            in_specs=[_rows(OUT_BLOCK, d)] + [_rows(OUT_BLOCK, WIDTH)] * 3
            + [_full((3 * WIDTH, d)), _full((1, d))],
            out_specs=_rows(OUT_BLOCK, d),
            out_shape=jax.ShapeDtypeStruct((seq, d), F32),
            compiler_params=pltpu.CompilerParams(dimension_semantics=("arbitrary",),
                                                 vmem_limit_bytes=VMEM_LIMIT),
            name=f"outproj_{l}",
        )(x2, y_a, y_b, y_c, w_out[l].astype(BF16), _row(final_norm_w))

    return x2.reshape(bsz, seq, d)
```

```python
import functools

import numpy as np
import jax
import jax.numpy as jnp
from jax import lax
from jax.experimental import pallas as pl
from jax.experimental.pallas import tpu as pltpu

F32 = jnp.float32
BF16 = jnp.bfloat16

HEAD_DIM = 64
HEADS = 8
PAIRS = HEADS // 2
WIDTH = HEADS * HEAD_DIM
CHUNK = 64
LEVELS = CHUNK.bit_length() - 1
CONV_WIDTH = 4
LORA = 64
NORM_EPS = 1e-6
L2_EPS = 1e-6
RWKV_GN_EPS = 64e-5
LANES = 128
SUBLANES = 8
RWKV_PROJ = 4 * WIDTH + 2 * LORA
HGRN_PROJ = 4 * WIDTH
TIME_BLOCK = 256
PREP_CHUNKS = 2
OUT_BLOCK = 512
VMEM_LIMIT = 56 * 1024 * 1024


def _dot(a, b):
    return jnp.dot(a, b, preferred_element_type=F32)


def _dot_nt(a, b):
    return lax.dot_general(a, b, (((1,), (1,)), ((), ())), preferred_element_type=F32)


def _dot_tn(a, b):
    return lax.dot_general(a, b, (((0,), (0,)), ((), ())), preferred_element_type=F32)


def _split_bf16(x, n):
    parts, r = [], x
    for i in range(n):
        p = r.astype(BF16)
        parts.append(p)
        if i + 1 < n:
            r = r - p.astype(F32)
    return parts


def _sel_dot(m01, x, n):
    out = None
    for p in _split_bf16(x, n):
        t = _dot(m01, p)
        out = t if out is None else out + t
    return out


def _dot_sel(x, m01, n):
    out = None
    for p in _split_bf16(x, n):
        t = _dot(p, m01)
        out = t if out is None else out + t
    return out


def _iota(shape, dim):
    return lax.broadcasted_iota(jnp.int32, shape, dim)


def _bd_inverse_all(ps, eye_bd):
    xs = [eye_bd + p for p in ps]
    ps = [_dot(p, p) for p in ps]
    for _ in range(LEVELS - 2):
        r = [_dot(p, jnp.concatenate([x, p], axis=1)) for p, x in zip(ps, xs)]
        xs = [x + t[:, 0:LANES] for x, t in zip(xs, r)]
        ps = [t[:, LANES:] for t in r]
    return [x + _dot(p, x) for p, x in zip(ps, xs)]


def _pair_masks():
    i = _iota((CHUNK, LANES), 0)
    c = _iota((CHUNK, LANES), 1)
    cm = c & (HEAD_DIM - 1)
    r = _iota((LANES, LANES), 0)
    l = _iota((LANES, LANES), 1)
    return dict(lo=c < HEAD_DIM, causal=cm <= i, strict=cm < i,
                bd=(r // HEAD_DIM) == (l // HEAD_DIM), eye=jnp.where(r == l, 1.0, 0.0))


def _stack(x, lo):
    return jnp.concatenate([jnp.where(lo, x, 0.0), jnp.where(lo, 0.0, x)], axis=0)


def _pair_tiles():
    return [slice(p * LANES, (p + 1) * LANES) for p in range(PAIRS)]


def _silu(x):
    return x * jax.nn.sigmoid(x)


def _norm_proj(x_ref, nw_ref, w_ref):
    x = x_ref[...]
    ms = jnp.mean(x * x, axis=-1, keepdims=True)
    h = (x * lax.rsqrt(ms + NORM_EPS)) * nw_ref[...]
    return _dot(h.astype(BF16), w_ref[...])


def _np_head_expand(offset):
    r = np.arange(LANES)[:, None]
    c = np.arange(WIDTH)[None, :]
    return r == c // HEAD_DIM + offset


def _np_head_reduce():
    return _np_head_expand(0).T


def _np_chunk_cumsum(n):
    r = np.arange(n)[:, None]
    c = np.arange(n)[None, :]
    return (r // CHUNK == c // CHUNK) & (c <= r)


def _np_level_matrix(n, s):
    r = np.arange(n)[:, None]
    c = np.arange(n)[None, :]
    tr, tc = r % CHUNK, c % CHUNK
    mid = (tr // (2 * s)) * (2 * s) + s - 1
    upper = (tr // s) % 2 == 1
    m = (upper & (tc > mid) & (tc <= tr)) | (~upper & (tc > tr) & (tc <= mid))
    return (r // CHUNK == c // CHUNK) & m


def _const(mask):
    return jnp.asarray(np.asarray(mask, np.float32), BF16)


def _gdn_kernel(x_ref, nw_ref, w_ref, conv_ref, alog_ref, dtb_ref, gnw_ref, cum_ref, red_ref, expb_ref,
                expg_ref, y_ref,
                s_ref, ext_ref, q_ref, k_ref, kb_ref, vb_ref, kbe_ref, qd_ref, gcf_ref, gcs_ref,
                qm_ref, n_ref, o_ref):
    tb = x_ref.shape[0]

    @pl.when(pl.program_id(0) == 0)
    def _():
        s_ref[...] = jnp.zeros_like(s_ref)
        ext_ref[0:SUBLANES, :] = jnp.zeros((SUBLANES, 3 * WIDTH), F32)

    p = _norm_proj(x_ref, nw_ref, w_ref)
    gate = p[:, 3 * WIDTH:4 * WIDTH]
    small = p[:, 4 * WIDTH:4 * WIDTH + LANES]

    ext_ref[SUBLANES:SUBLANES + tb, :] = p[:, 0:3 * WIDTH]
    conv = None
    for j in range(CONV_WIDTH):
        t = conv_ref[j:j + 1, :] * ext_ref[pl.ds(SUBLANES - (CONV_WIDTH - 1) + j, tb), :]
        conv = t if conv is None else conv + t
    ext_ref[0:SUBLANES, :] = ext_ref[tb:tb + SUBLANES, :]
    qkv = _silu(conv)
    q, k, v = qkv[:, 0:WIDTH], qkv[:, WIDTH:2 * WIDTH], qkv[:, 2 * WIDTH:3 * WIDTH]

    red = red_ref[...]
    exp_b = expb_ref[...]
    q_rs = lax.rsqrt(_dot_sel(q * q, red, 1) + L2_EPS)
    k_rs = lax.rsqrt(_dot_sel(k * k, red, 1) + L2_EPS)
    qn = (q * _dot_sel(q_rs, exp_b, 2)) * (HEAD_DIM ** -0.5)
    kn = k * _dot_sel(k_rs, exp_b, 2)

    beta = _dot_sel(jax.nn.sigmoid(small), exp_b, 2)
    g_small = -jnp.exp(alog_ref[...]) * jax.nn.softplus(small + dtb_ref[...])
    gc_small = _sel_dot(cum_ref[...], g_small, 3)
    gc_full = _dot_sel(gc_small, expg_ref[...], 3)
    egc = jnp.exp(gc_full)
    kb = kn * beta

    q_ref[...] = qn
    k_ref[...] = kn
    kb_ref[...] = kb
    vb_ref[...] = v * beta
    kbe_ref[...] = kb * egc
    qd_ref[...] = qn * egc
    gcf_ref[...] = gc_full
    gcs_ref[...] = gc_small

    mk = _pair_masks()
    lo, causal, strict, bd, eye = mk["lo"], mk["causal"], mk["strict"], mk["bd"], mk["eye"]
    tiles = _pair_tiles()

    def prep_body(i, carry):
        cs = [i * PREP_CHUNKS + j for j in range(PREP_CHUNKS)]
        rws = [pl.ds(pl.multiple_of(c * CHUNK, CHUNK), CHUNK) for c in cs]
        gc_c = [gcf_ref[rows, :] for rows in rws]
        gl = [gcf_ref[pl.ds(pl.multiple_of(c * CHUNK, CHUNK) + CHUNK - 1, 1), :] for c in cs]
        kd_c = [k_ref[rows, :] * jnp.exp(g1 - g2) for rows, g1, g2 in zip(rws, gl, gc_c)]
        last = [jnp.exp(g1) for g1 in gl]
        g_rows = [gcs_ref[rows, :].T for rows in rws]
        items = [(j, p) for j in range(PREP_CHUNKS) for p in range(PAIRS)]
        ak = [_dot_nt(jnp.concatenate([kb_ref[rws[j], tiles[p]], q_ref[rws[j], tiles[p]]], axis=0),
                      _stack(k_ref[rws[j], tiles[p]], lo)) for j, p in items]
        g_row = [jnp.concatenate([g_rows[j][HEADS + 2 * p:HEADS + 2 * p + 1, :],
                                  g_rows[j][HEADS + 2 * p + 1:HEADS + 2 * p + 2, :]], axis=1)
                 for j, p in items]
        dm = [jnp.where(causal, jnp.exp(jnp.where(causal, gc_c[j][:, tiles[p]] - g, 0.0)), 0.0)
              for (j, p), g in zip(items, g_row)]
        qk = [t[CHUNK:] * d for t, d in zip(ak, dm)]
        tm = _bd_inverse_all([_stack(jnp.where(strict, -(t[0:CHUNK] * d), 0.0), lo)
                              for t, d in zip(ak, dm)], eye)
        uw = [_dot(t, jnp.concatenate([_stack(vb_ref[rws[j], tiles[p]], lo),
                                       _stack(kbe_ref[rws[j], tiles[p]], lo)], axis=1))
              for (j, p), t in zip(items, tm)]
        t1 = [_dot(a, b) for a, b in zip(qk, uw)]
        t2 = [_dot_tn(kd_c[j][:, tiles[p]], b[0:CHUNK] + b[CHUNK:])
              for (j, p), b in zip(items, uw)]
        for n, (j, p) in enumerate(items):
            qm_ref[cs[j], p, 0:CHUNK, :] = qd_ref[rws[j], tiles[p]] - t1[n][:, LANES:]
            qm_ref[cs[j], p, CHUNK:, :] = eye * last[j][:, tiles[p]] - jnp.where(bd, t2[n][:, LANES:], 0.0)
            n_ref[cs[j], p] = jnp.where(bd, t2[n][:, 0:LANES], 0.0)
            o_ref[rws[j], tiles[p]] = t1[n][:, 0:LANES]
        return carry

    lax.fori_loop(0, tb // (CHUNK * PREP_CHUNKS), prep_body, 0)

    def scan_body(c, carry):
        rows = pl.ds(pl.multiple_of(c * CHUNK, CHUNK), CHUNK)
        out = [_dot(qm_ref[c, p], s_ref[p]) for p in range(PAIRS)]
        for p in range(PAIRS):
            s_ref[p] = out[p][CHUNK:] + n_ref[c, p]
            o_ref[rows, tiles[p]] = o_ref[rows, tiles[p]] + out[p][0:CHUNK]
        return carry

    lax.fori_loop(0, tb // CHUNK, scan_body, 0)

    o = o_ref[...]
    rs = lax.rsqrt(_dot_sel(o * o, red, 1) * (1.0 / HEAD_DIM) + NORM_EPS)
    y_ref[...] = (o * _dot_sel(rs, exp_b, 2)) * gnw_ref[...] * _silu(gate)


def _rwkv_kernel(x_ref, nw_ref, w_ref, mu_ref, w0_ref, wup_ref, a0_ref, aup_ref, kk_ref, ka_ref,
                 rk_ref, lnw_ref, lnb_ref, cum_ref, red_ref, expd_ref, y_ref,
                 s_ref, ext_ref, qt_ref, rt_ref, ph_ref, kh_ref, pa_ref, k2_ref, v_ref, b_ref,
                 qm_ref, n_ref, o_ref):
    tb = x_ref.shape[0]

    @pl.when(pl.program_id(0) == 0)
    def _():
        s_ref[...] = jnp.zeros_like(s_ref)
        ext_ref[0:SUBLANES, :] = jnp.zeros((SUBLANES, RWKV_PROJ), F32)

    p = _norm_proj(x_ref, nw_ref, w_ref)
    ext_ref[SUBLANES:SUBLANES + tb, :] = p
    prev = ext_ref[pl.ds(SUBLANES - 1, tb), :]
    ext_ref[0:SUBLANES, :] = ext_ref[tb:tb + SUBLANES, :]
    p = p + mu_ref[...] * (prev - p)
    r, k, v = p[:, 0:WIDTH], p[:, WIDTH:2 * WIDTH], p[:, 2 * WIDTH:3 * WIDTH]
    wd = p[:, 3 * WIDTH:3 * WIDTH + LORA]
    ad = p[:, 3 * WIDTH + LORA:3 * WIDTH + 2 * LORA]
    gate = p[:, 3 * WIDTH + 2 * LORA:]

    w_raw = -jax.nn.softplus(-(w0_ref[...] + _dot(jnp.tanh(wd), wup_ref[...]))) - 0.5
    logw = -jnp.exp(w_raw)
    a = jax.nn.sigmoid(a0_ref[...] + _dot(ad, aup_ref[...]))

    red = red_ref[...]
    expd = expd_ref[...]
    kkp = k * kk_ref[...]
    kk = kkp * _dot_sel(lax.rsqrt(_dot_sel(kkp * kkp, red, 1) + L2_EPS), expd, 2)
    k2 = k * (1.0 + (a - 1.0) * ka_ref[...])
    pa = kk * a

    b = _sel_dot(cum_ref[...], logw, 2)
    enb = jnp.exp(-b)
    qt_ref[...] = -kk * jnp.exp(b - logw)
    rt_ref[...] = r * jnp.exp(b)
    ph_ref[...] = pa * enb
    kh_ref[...] = k2 * enb
    pa_ref[...] = pa
    k2_ref[...] = k2
    v_ref[...] = v
    b_ref[...] = b

    mk = _pair_masks()
    lo, causal, strict, bd, eye = mk["lo"], mk["causal"], mk["strict"], mk["bd"], mk["eye"]
    tiles = _pair_tiles()

    def prep_body(i, carry):
        cs = [i * PREP_CHUNKS + j for j in range(PREP_CHUNKS)]
        rws = [pl.ds(pl.multiple_of(c * CHUNK, CHUNK), CHUNK) for c in cs]
        bl = [b_ref[pl.ds(pl.multiple_of(c * CHUNK, CHUNK) + CHUNK - 1, 1), :] for c in cs]
        ebl = [jnp.exp(b1 - b_ref[rows, :]) for b1, rows in zip(bl, rws)]
        pt_c = [pa_ref[rows, :] * e for rows, e in zip(rws, ebl)]
        kt_c = [k2_ref[rows, :] * e for rows, e in zip(rws, ebl)]
        wl = [jnp.exp(b1) for b1 in bl]
        items = [(j, p) for j in range(PREP_CHUNKS) for p in range(PAIRS)]
        qt = [qt_ref[rws[j], tiles[p]] for j, p in items]
        rt = [rt_ref[rws[j], tiles[p]] for j, p in items]
        v_t = [v_ref[rws[j], tiles[p]] for j, p in items]
        qr = [jnp.concatenate([a, b], axis=0) for a, b in zip(qt, rt)]
        aap = [_dot_nt(t, _stack(ph_ref[rws[j], tiles[p]], lo)) for (j, p), t in zip(items, qr)]
        aak = [_dot_nt(t, _stack(kh_ref[rws[j], tiles[p]], lo)) for (j, p), t in zip(items, qr)]
        av = [_dot(jnp.concatenate([jnp.where(strict, t[0:CHUNK], 0.0), jnp.where(causal, t[CHUNK:], 0.0)],
                                   axis=0), _stack(v, lo)) for t, v in zip(aak, v_t)]
        tinv = _bd_inverse_all([_stack(jnp.where(strict, t[0:CHUNK], 0.0), lo) for t in aap], eye)
        qu = [_dot(t, jnp.concatenate([_stack(a, lo), _stack(b[0:CHUNK], lo)], axis=1))
              for t, a, b in zip(tinv, qt, av)]
        t1 = [_dot(jnp.where(causal, t[CHUNK:], 0.0), y) for t, y in zip(aap, qu)]
        qu_t = [y[0:CHUNK] + y[CHUNK:] for y in qu]
        t2 = [_dot_tn(pt_c[j][:, tiles[p]], y[:, 0:LANES]) for (j, p), y in zip(items, qu_t)]
        t3 = [_dot_tn(jnp.concatenate([pt_c[j][:, tiles[p]], kt_c[j][:, tiles[p]]], axis=0),
                      jnp.concatenate([y[:, LANES:], v], axis=0))
              for (j, p), y, v in zip(items, qu_t, v_t)]
        for n, (j, p) in enumerate(items):
            qm_ref[cs[j], p, 0:CHUNK, :] = rt[n] + t1[n][:, 0:LANES]
            qm_ref[cs[j], p, CHUNK:, :] = eye * wl[j][:, tiles[p]] + jnp.where(bd, t2[n], 0.0)
            n_ref[cs[j], p] = jnp.where(bd, t3[n], 0.0)
            o_ref[rws[j], tiles[p]] = t1[n][:, LANES:] + av[n][CHUNK:]
        return carry

    lax.fori_loop(0, tb // (CHUNK * PREP_CHUNKS), prep_body, 0)

    def scan_body(c, carry):
        rows = pl.ds(pl.multiple_of(c * CHUNK, CHUNK), CHUNK)
        out = [_dot(qm_ref[c, p], s_ref[p]) for p in range(PAIRS)]
        for p in range(PAIRS):
            s_ref[p] = out[p][CHUNK:] + n_ref[c, p]
            o_ref[rows, tiles[p]] = o_ref[rows, tiles[p]] + out[p][0:CHUNK]
        return carry

    lax.fori_loop(0, tb // CHUNK, scan_body, 0)

    o = o_ref[...]
    inv_d = 1.0 / HEAD_DIM
    mean = _dot_sel(_dot_sel(o, red, 2) * inv_d, expd, 2)
    d = o - mean
    var = _dot_sel(d * d, red, 1) * inv_d
    on = d * _dot_sel(lax.rsqrt(var + RWKV_GN_EPS), expd, 2)
    on = on * lnw_ref[...] + lnb_ref[...]
    bonus = _dot_sel(_dot_sel(r * k2 * rk_ref[...], red, 2), expd, 2) * v
    y_ref[...] = (on + bonus) * _silu(gate)


def _hgrn_kernel(x_ref, nw_ref, w_ref, lbraw_ref, hnw_ref, cum_ref, lvl_ref, red_ref, expd_ref, y_ref,
                 s_ref, kin_ref, v_ref, b_ref, qe_ref, ql_ref, kl_ref, o_ref, *, layer, depth):
    tb = x_ref.shape[0]

    @pl.when(pl.program_id(0) == 0)
    def _():
        s_ref[...] = jnp.zeros_like(s_ref)

    lbs = [lbraw_ref[j:j + 1, :] for j in range(depth)]
    mx = lbs[0]
    for t in lbs[1:]:
        mx = jnp.maximum(mx, t)
    es = [jnp.exp(t - mx) for t in lbs]
    tot = es[0]
    for t in es[1:]:
        tot = tot + t
    soft = [t / tot for t in es]
    lb = soft[0]
    for t in soft[1:layer + 1]:
        lb = lb + t
    lb = lb - soft[0]

    p = _norm_proj(x_ref, nw_ref, w_ref)
    q = _silu(p[:, 0:WIDTH])
    f = p[:, WIDTH:2 * WIDTH]
    v = p[:, 2 * WIDTH:3 * WIDTH]
    gate = p[:, 3 * WIDTH:]
    log_g = jax.nn.log_sigmoid(f) + jnp.log1p(lb * jnp.exp(-f))
    kin = (1.0 - lb) * jax.nn.sigmoid(-f)

    b = _sel_dot(cum_ref[...], log_g, 2)
    kin_ref[...] = kin
    v_ref[...] = v
    b_ref[...] = b
    qe_ref[...] = q * jnp.exp(b)
    lg_parts = _split_bf16(log_g, 2)
    tr = _iota((tb, WIDTH), 0)
    for l in range(LEVELS):
        e = jnp.exp(_dot(lvl_ref[l], lg_parts[0]) + _dot(lvl_ref[l], lg_parts[1]))
        upper = (tr & (1 << l)) != 0
        ql_ref[l] = jnp.where(upper, q * e, 0.0)
        kl_ref[l] = jnp.where(upper, 0.0, kin * e)

    mk = _pair_masks()
    lo, bd = mk["lo"], mk["bd"]
    ri = _iota((CHUNK, LANES), 0)
    ci = _iota((CHUNK, LANES), 1) & (HEAD_DIM - 1)
    lvl_masks = [((ri >> (l + 1)) == (ci >> (l + 1))) & ((ri & (1 << l)) != 0) & ((ci & (1 << l)) == 0)
                 for l in range(LEVELS)]
    tiles = _pair_tiles()

    def chunk_body(i, carry):
        cs = [i * PREP_CHUNKS + j for j in range(PREP_CHUNKS)]
        rws = [pl.ds(pl.multiple_of(c * CHUNK, CHUNK), CHUNK) for c in cs]
        bl = [b_ref[pl.ds(pl.multiple_of(c * CHUNK, CHUNK) + CHUNK - 1, 1), :] for c in cs]
        kt_c = [kin_ref[rows, :] * jnp.exp(b1 - b_ref[rows, :]) for b1, rows in zip(bl, rws)]
        wl = [jnp.exp(b1) for b1 in bl]
        items = [(j, p) for j in range(PREP_CHUNKS) for p in range(PAIRS)]
        att = [None] * len(items)
        for l in range(LEVELS):
            for n, (j, p) in enumerate(items):
                t = jnp.where(lvl_masks[l], _dot_nt(ql_ref[l, rws[j], tiles[p]],
                                                    _stack(kl_ref[l, rws[j], tiles[p]], lo)), 0.0)
                att[n] = t if att[n] is None else att[n] + t
        v_t = [v_ref[rws[j], tiles[p]] for j, p in items]
        oi = [_dot(a, _stack(v, lo)) for a, v in zip(att, v_t)]
        sn = [jnp.where(bd, _dot_tn(v, kt_c[j][:, tiles[p]]), 0.0) for (j, p), v in zip(items, v_t)]
        st = [s_ref[p] for p in range(PAIRS)]
        for n, (j, p) in enumerate(items):
            o_ref[rws[j], tiles[p]] = oi[n] + _dot_nt(qe_ref[rws[j], tiles[p]], st[p])
            st[p] = st[p] * wl[j][:, tiles[p]] + sn[n]
        for p in range(PAIRS):
            s_ref[p] = st[p]
        return carry

    lax.fori_loop(0, tb // (CHUNK * PREP_CHUNKS), chunk_body, 0)

    o = o_ref[...] + _dot_sel(_dot_sel(q * kin, red_ref[...], 2), expd_ref[...], 2) * v
    ms = jnp.mean(o * o, axis=-1, keepdims=True)
    y_ref[...] = (o * lax.rsqrt(ms + NORM_EPS)) * hnw_ref[...] * _silu(gate)


def _outproj_kernel(x_ref, ya_ref, yb_ref, yc_ref, w_ref, fnw_ref, o_ref, *, final):
    acc = x_ref[...]
    for i, y_ref in enumerate((ya_ref, yb_ref, yc_ref)):
        acc = acc + _dot(y_ref[...].astype(BF16), w_ref[i * WIDTH:(i + 1) * WIDTH, :])
    if final:
        ms = jnp.mean(acc * acc, axis=-1, keepdims=True)
        acc = (acc * lax.rsqrt(ms + NORM_EPS)) * fnw_ref[...]
    o_ref[...] = acc


def _full(shape):
    nd = len(shape)
    return pl.BlockSpec(shape, lambda i, _nd=nd: (0,) * _nd)


def _rows(tb, width):
    return pl.BlockSpec((tb, width), lambda i: (i, 0))


def _mixer_call(body, name, x2, params, scratch):
    t, d = x2.shape
    tb = TIME_BLOCK
    return pl.pallas_call(
        body,
        grid=(t // tb,),
        in_specs=[_rows(tb, d)] + [_full(a.shape) for a in params],
        out_specs=_rows(tb, WIDTH),
        out_shape=jax.ShapeDtypeStruct((t, WIDTH), F32),
        scratch_shapes=scratch,
        compiler_params=pltpu.CompilerParams(dimension_semantics=("arbitrary",),
                                             vmem_limit_bytes=VMEM_LIMIT),
        name=name,
    )(x2, *params)


def _slab(tb):
    return pltpu.VMEM((tb, WIDTH), F32)


def _state():
    return pltpu.VMEM((PAIRS, LANES, LANES), F32)


def _folded(nc):
    return [pltpu.VMEM((nc, PAIRS, CHUNK + LANES, LANES), F32), pltpu.VMEM((nc, PAIRS, LANES, LANES), F32)]


def _row(a):
    return a.reshape(1, -1).astype(F32)


def _lane_pad(a, offset):
    return jnp.zeros((1, LANES), F32).at[0, offset:offset + HEADS].set(a.astype(F32))


def kernel(x, norm_w, w_in, gdn_conv_w, gdn_a_log, gdn_dt_bias, gdn_norm_w, rwkv_mu, rwkv_w0, rwkv_w_up, rwkv_a0, rwkv_a_up, rwkv_k_k, rwkv_k_a, rwkv_r_k, rwkv_ln_w, rwkv_ln_b, hgrn_lower_bounds, hgrn_norm_w, w_out, final_norm_w):
    bsz, seq, d = x.shape
    depth = norm_w.shape[0]
    assert bsz == 1 and seq % TIME_BLOCK == 0 and seq % OUT_BLOCK == 0
    tb = TIME_BLOCK
    nc = tb // CHUNK
    x2 = x.reshape(seq, d)

    gdn_cols = 3 * WIDTH + 2 * HEADS + WIDTH
    rwkv_off = gdn_cols
    hgrn_off = gdn_cols + RWKV_PROJ

    cum = _const(_np_chunk_cumsum(tb))
    red = _const(_np_head_reduce())
    exp0 = _const(_np_head_expand(0))
    exp8 = _const(_np_head_expand(HEADS))
    lvl = _const(np.stack([_np_level_matrix(tb, 1 << l) for l in range(LEVELS)]))

    for l in range(depth):
        wl = w_in[l]
        nw = _row(norm_w[l])
        w_gdn = jnp.concatenate(
            [wl[:, 0:3 * WIDTH], wl[:, 3 * WIDTH + 2 * HEADS:gdn_cols],
             wl[:, 3 * WIDTH:3 * WIDTH + 2 * HEADS],
             jnp.zeros((d, LANES - 2 * HEADS), wl.dtype)], axis=1).astype(BF16)
        w_rwkv = wl[:, rwkv_off:rwkv_off + RWKV_PROJ].astype(BF16)
        w_hgrn = wl[:, hgrn_off:hgrn_off + HGRN_PROJ].astype(BF16)

        y_a = _mixer_call(
            _gdn_kernel, f"gdn_{l}", x2,
            [nw, w_gdn, gdn_conv_w[l].astype(F32), _lane_pad(gdn_a_log[l], HEADS),
             _lane_pad(gdn_dt_bias[l], HEADS), _row(jnp.tile(gdn_norm_w[l], HEADS)),
             cum, red, exp0, exp8],
            [_state(), pltpu.VMEM((tb + SUBLANES, 3 * WIDTH), F32)] + [_slab(tb)] * 7
            + [pltpu.VMEM((tb, LANES), F32)] + _folded(nc) + [_slab(tb)])

        y_b = _mixer_call(
            _rwkv_kernel, f"rwkv_{l}", x2,
            [nw, w_rwkv, _row(rwkv_mu[l]), _row(rwkv_w0[l]), rwkv_w_up[l].astype(F32),
             _row(rwkv_a0[l]), rwkv_a_up[l].astype(F32), _row(rwkv_k_k[l]), _row(rwkv_k_a[l]),
             _row(rwkv_r_k[l]), _row(rwkv_ln_w[l]), _row(rwkv_ln_b[l]), cum, red, exp0],
            [_state(), pltpu.VMEM((tb + SUBLANES, RWKV_PROJ), F32)] + [_slab(tb)] * 8
            + _folded(nc) + [_slab(tb)])

        y_c = _mixer_call(
            functools.partial(_hgrn_kernel, layer=l, depth=depth), f"hgrn_{l}", x2,
            [nw, w_hgrn, hgrn_lower_bounds.astype(F32), _row(hgrn_norm_w[l]), cum, lvl, red, exp0],
            [_state()] + [_slab(tb)] * 4
            + [pltpu.VMEM((LEVELS, tb, WIDTH), F32)] * 2 + [_slab(tb)])

        final = l == depth - 1
        x2 = pl.pallas_call(
            functools.partial(_outproj_kernel, final=final),
            grid=(seq // OUT_BLOCK,),
            in_specs=[_rows(OUT_BLOCK, d)] + [_rows(OUT_BLOCK, WIDTH)] * 3
            + [_full((3 * WIDTH, d)), _full((1, d))],
            out_specs=_rows(OUT_BLOCK, d),
            out_shape=jax.ShapeDtypeStruct((seq, d), F32),
            compiler_params=pltpu.CompilerParams(dimension_semantics=("arbitrary",),
                                                 vmem_limit_bytes=VMEM_LIMIT),
            name=f"outproj_{l}",
        )(x2, y_a, y_b, y_c, w_out[l].astype(BF16), _row(final_norm_w))

    return x2.reshape(bsz, seq, d)
```

```python
import functools

import numpy as np
import jax
import jax.numpy as jnp
from jax import lax
from jax.experimental import pallas as pl
from jax.experimental.pallas import tpu as pltpu

F32 = jnp.float32
BF16 = jnp.bfloat16

HEAD_DIM = 64
HEADS = 8
PAIRS = HEADS // 2
WIDTH = HEADS * HEAD_DIM
CHUNK = 64
LEVELS = CHUNK.bit_length() - 1
CONV_WIDTH = 4
LORA = 64
NORM_EPS = 1e-6
L2_EPS = 1e-6
RWKV_GN_EPS = 64e-5
LANES = 128
SUBLANES = 8
RWKV_PROJ = 4 * WIDTH + 2 * LORA
HGRN_PROJ = 4 * WIDTH
TIME_BLOCK = 512
CONST_ROWS = 256
PREP_CHUNKS = 2
HGRN_CHUNKS = 4
SUM_PASSES = 1
EXPAND_PASSES = 1
CUMSUM_PASSES = 2
GDN_DECAY_PASSES = 3
OUT_BLOCK = 512
VMEM_LIMIT = 56 * 1024 * 1024


def _dot(a, b):
    return jnp.dot(a, b, preferred_element_type=F32)


def _dot_nt(a, b):
    return lax.dot_general(a, b, (((1,), (1,)), ((), ())), preferred_element_type=F32)


def _dot_tn(a, b):
    return lax.dot_general(a, b, (((0,), (0,)), ((), ())), preferred_element_type=F32)


def _split_bf16(x, n):
    parts, r = [], x
    for i in range(n):
        p = r.astype(BF16)
        parts.append(p)
        if i + 1 < n:
            r = r - p.astype(F32)
    return parts


def _sel_dot(m01, x, n):
    parts = _split_bf16(x, n)
    blocks = []
    for r0 in range(0, x.shape[0], CONST_ROWS):
        out = None
        for p in parts:
            t = _dot(m01, p[r0:r0 + CONST_ROWS])
            out = t if out is None else out + t
        blocks.append(out)
    return blocks[0] if len(blocks) == 1 else jnp.concatenate(blocks, axis=0)


def _dot_sel(x, m01, n):
    out = None
    for p in _split_bf16(x, n):
        t = _dot(p, m01)
        out = t if out is None else out + t
    return out


def _iota(shape, dim):
    return lax.broadcasted_iota(jnp.int32, shape, dim)


def _bd_inverse_all(ps, eye_bd):
    xs = [eye_bd + p for p in ps]
    ps = [_dot(p, p) for p in ps]
    for _ in range(LEVELS - 2):
        r = [_dot(p, jnp.concatenate([x, p], axis=1)) for p, x in zip(ps, xs)]
        xs = [x + t[:, 0:LANES] for x, t in zip(xs, r)]
        ps = [t[:, LANES:] for t in r]
    return [x + _dot(p, x) for p, x in zip(ps, xs)]


def _pair_masks():
    i = _iota((CHUNK, LANES), 0)
    c = _iota((CHUNK, LANES), 1)
    cm = c & (HEAD_DIM - 1)
    r = _iota((LANES, LANES), 0)
    l = _iota((LANES, LANES), 1)
    return dict(lo=c < HEAD_DIM, causal=cm <= i, strict=cm < i,
                bd=(r // HEAD_DIM) == (l // HEAD_DIM), eye=jnp.where(r == l, 1.0, 0.0))


def _stack(x, lo):
    return jnp.concatenate([jnp.where(lo, x, 0.0), jnp.where(lo, 0.0, x)], axis=0)


def _pair_tiles():
    return [slice(p * LANES, (p + 1) * LANES) for p in range(PAIRS)]


def _silu(x):
    return x * jax.nn.sigmoid(x)


def _norm_proj(x_ref, nw_ref, w_ref):
    x = x_ref[...]
    ms = jnp.mean(x * x, axis=-1, keepdims=True)
    h = (x * lax.rsqrt(ms + NORM_EPS)) * nw_ref[...]
    return _dot(h.astype(BF16), w_ref[...])


def _np_head_expand(offset):
    r = np.arange(LANES)[:, None]
    c = np.arange(WIDTH)[None, :]
    return r == c // HEAD_DIM + offset


def _np_head_reduce():
    return _np_head_expand(0).T


def _np_chunk_cumsum(n):
    r = np.arange(n)[:, None]
    c = np.arange(n)[None, :]
    return (r // CHUNK == c // CHUNK) & (c <= r)


def _np_level_matrix(n, s):
    r = np.arange(n)[:, None]
    c = np.arange(n)[None, :]
    tr, tc = r % CHUNK, c % CHUNK
    mid = (tr // (2 * s)) * (2 * s) + s - 1
    upper = (tr // s) % 2 == 1
    m = (upper & (tc > mid) & (tc <= tr)) | (~upper & (tc > tr) & (tc <= mid))
    return (r // CHUNK == c // CHUNK) & m


def _const(mask):
    return jnp.asarray(np.asarray(mask, np.float32), BF16)


def _gdn_kernel(x_ref, nw_ref, w_ref, conv_ref, alog_ref, dtb_ref, gnw_ref, cum_ref, red_ref, expb_ref,
                expg_ref, y_ref,
                s_ref, ext_ref, q_ref, k_ref, kb_ref, vb_ref, kbe_ref, qd_ref, gcf_ref, gcs_ref,
                qm_ref, n_ref, o_ref):
    tb = x_ref.shape[0]

    @pl.when(pl.program_id(0) == 0)
    def _():
        s_ref[...] = jnp.zeros_like(s_ref)
        ext_ref[0:SUBLANES, :] = jnp.zeros((SUBLANES, 3 * WIDTH), F32)

    p = _norm_proj(x_ref, nw_ref, w_ref)
    gate = p[:, 3 * WIDTH:4 * WIDTH]
    small = p[:, 4 * WIDTH:4 * WIDTH + LANES]

    ext_ref[SUBLANES:SUBLANES + tb, :] = p[:, 0:3 * WIDTH]
    conv = None
    for j in range(CONV_WIDTH):
        t = conv_ref[j:j + 1, :] * ext_ref[pl.ds(SUBLANES - (CONV_WIDTH - 1) + j, tb), :]
        conv = t if conv is None else conv + t
    ext_ref[0:SUBLANES, :] = ext_ref[tb:tb + SUBLANES, :]
    qkv = _silu(conv)
    q, k, v = qkv[:, 0:WIDTH], qkv[:, WIDTH:2 * WIDTH], qkv[:, 2 * WIDTH:3 * WIDTH]

    red = red_ref[...]
    exp_b = expb_ref[...]
    q_rs = lax.rsqrt(_dot_sel(q * q, red, SUM_PASSES) + L2_EPS)
    k_rs = lax.rsqrt(_dot_sel(k * k, red, SUM_PASSES) + L2_EPS)
    qn = (q * _dot_sel(q_rs, exp_b, EXPAND_PASSES)) * (HEAD_DIM ** -0.5)
    kn = k * _dot_sel(k_rs, exp_b, EXPAND_PASSES)

    beta = _dot_sel(jax.nn.sigmoid(small), exp_b, EXPAND_PASSES)
    g_small = -jnp.exp(alog_ref[...]) * jax.nn.softplus(small + dtb_ref[...])
    gc_small = _sel_dot(cum_ref[...], g_small, GDN_DECAY_PASSES)
    gc_full = _dot_sel(gc_small, expg_ref[...], GDN_DECAY_PASSES)
    egc = jnp.exp(gc_full)
    kb = kn * beta

    q_ref[...] = qn
    k_ref[...] = kn
    kb_ref[...] = kb
    vb_ref[...] = v * beta
    kbe_ref[...] = kb * egc
    qd_ref[...] = qn * egc
    gcf_ref[...] = gc_full
    gcs_ref[...] = gc_small

    mk = _pair_masks()
    lo, causal, strict, bd, eye = mk["lo"], mk["causal"], mk["strict"], mk["bd"], mk["eye"]
    tiles = _pair_tiles()

    def prep_body(i, carry):
        cs = [i * PREP_CHUNKS + j for j in range(PREP_CHUNKS)]
        rws = [pl.ds(pl.multiple_of(c * CHUNK, CHUNK), CHUNK) for c in cs]
        gc_c = [gcf_ref[rows, :] for rows in rws]
        gl = [gcf_ref[pl.ds(pl.multiple_of(c * CHUNK, CHUNK) + CHUNK - 1, 1), :] for c in cs]
        kd_c = [k_ref[rows, :] * jnp.exp(g1 - g2) for rows, g1, g2 in zip(rws, gl, gc_c)]
        last = [jnp.exp(g1) for g1 in gl]
        g_rows = [gcs_ref[rows, :].T for rows in rws]
        items = [(j, p) for j in range(PREP_CHUNKS) for p in range(PAIRS)]
        ak = [_dot_nt(jnp.concatenate([kb_ref[rws[j], tiles[p]], q_ref[rws[j], tiles[p]]], axis=0),
                      _stack(k_ref[rws[j], tiles[p]], lo)) for j, p in items]
        g_row = [jnp.concatenate([g_rows[j][HEADS + 2 * p:HEADS + 2 * p + 1, :],
                                  g_rows[j][HEADS + 2 * p + 1:HEADS + 2 * p + 2, :]], axis=1)
                 for j, p in items]
        dm = [jnp.where(causal, jnp.exp(jnp.where(causal, gc_c[j][:, tiles[p]] - g, 0.0)), 0.0)
              for (j, p), g in zip(items, g_row)]
        qk = [t[CHUNK:] * d for t, d in zip(ak, dm)]
        tm = _bd_inverse_all([_stack(jnp.where(strict, -(t[0:CHUNK] * d), 0.0), lo)
                              for t, d in zip(ak, dm)], eye)
        uw = [_dot(t, jnp.concatenate([_stack(vb_ref[rws[j], tiles[p]], lo),
                                       _stack(kbe_ref[rws[j], tiles[p]], lo)], axis=1))
              for (j, p), t in zip(items, tm)]
        t1 = [_dot(a, b) for a, b in zip(qk, uw)]
        t2 = [_dot_tn(kd_c[j][:, tiles[p]], b[0:CHUNK] + b[CHUNK:])
              for (j, p), b in zip(items, uw)]
        for n, (j, p) in enumerate(items):
            qm_ref[cs[j], p, 0:CHUNK, :] = qd_ref[rws[j], tiles[p]] - t1[n][:, LANES:]
            qm_ref[cs[j], p, CHUNK:, :] = eye * last[j][:, tiles[p]] - jnp.where(bd, t2[n][:, LANES:], 0.0)
            n_ref[cs[j], p] = jnp.where(bd, t2[n][:, 0:LANES], 0.0)
            o_ref[rws[j], tiles[p]] = t1[n][:, 0:LANES]
        return carry

    lax.fori_loop(0, tb // (CHUNK * PREP_CHUNKS), prep_body, 0)

    def scan_body(c, carry):
        rows = pl.ds(pl.multiple_of(c * CHUNK, CHUNK), CHUNK)
        out = [_dot(qm_ref[c, p], s_ref[p]) for p in range(PAIRS)]
        for p in range(PAIRS):
            s_ref[p] = out[p][CHUNK:] + n_ref[c, p]
            o_ref[rows, tiles[p]] = o_ref[rows, tiles[p]] + out[p][0:CHUNK]
        return carry

    lax.fori_loop(0, tb // CHUNK, scan_body, 0)

    o = o_ref[...]
    rs = lax.rsqrt(_dot_sel(o * o, red_ref[...], SUM_PASSES) * (1.0 / HEAD_DIM) + NORM_EPS)
    y_ref[...] = (o * _dot_sel(rs, expb_ref[...], EXPAND_PASSES)) * gnw_ref[...] * _silu(gate)


def _rwkv_kernel(x_ref, nw_ref, w_ref, mu_ref, w0_ref, wup_ref, a0_ref, aup_ref, kk_ref, ka_ref,
                 rk_ref, lnw_ref, lnb_ref, cum_ref, red_ref, expd_ref, y_ref,
                 s_ref, ext_ref, qt_ref, rt_ref, ph_ref, kh_ref, pa_ref, k2_ref, v_ref, b_ref,
                 qm_ref, n_ref, o_ref):
    tb = x_ref.shape[0]

    @pl.when(pl.program_id(0) == 0)
    def _():
        s_ref[...] = jnp.zeros_like(s_ref)
        ext_ref[0:SUBLANES, :] = jnp.zeros((SUBLANES, RWKV_PROJ), F32)

    p = _norm_proj(x_ref, nw_ref, w_ref)
    ext_ref[SUBLANES:SUBLANES + tb, :] = p
    prev = ext_ref[pl.ds(SUBLANES - 1, tb), :]
    ext_ref[0:SUBLANES, :] = ext_ref[tb:tb + SUBLANES, :]
    p = p + mu_ref[...] * (prev - p)
    r, k, v = p[:, 0:WIDTH], p[:, WIDTH:2 * WIDTH], p[:, 2 * WIDTH:3 * WIDTH]
    wd = p[:, 3 * WIDTH:3 * WIDTH + LORA]
    ad = p[:, 3 * WIDTH + LORA:3 * WIDTH + 2 * LORA]
    gate = p[:, 3 * WIDTH + 2 * LORA:]

    w_raw = -jax.nn.softplus(-(w0_ref[...] + _dot(jnp.tanh(wd), wup_ref[...]))) - 0.5
    logw = -jnp.exp(w_raw)
    a = jax.nn.sigmoid(a0_ref[...] + _dot(ad, aup_ref[...]))

    red = red_ref[...]
    expd = expd_ref[...]
    kkp = k * kk_ref[...]
    kk = kkp * _dot_sel(lax.rsqrt(_dot_sel(kkp * kkp, red, SUM_PASSES) + L2_EPS), expd, EXPAND_PASSES)
    k2 = k * (1.0 + (a - 1.0) * ka_ref[...])
    pa = kk * a

    b = _sel_dot(cum_ref[...], logw, CUMSUM_PASSES)
    enb = jnp.exp(-b)
    qt_ref[...] = -kk * jnp.exp(b - logw)
    rt_ref[...] = r * jnp.exp(b)
    ph_ref[...] = pa * enb
    kh_ref[...] = k2 * enb
    pa_ref[...] = pa
    k2_ref[...] = k2
    v_ref[...] = v
    b_ref[...] = b

    mk = _pair_masks()
    lo, causal, strict, bd, eye = mk["lo"], mk["causal"], mk["strict"], mk["bd"], mk["eye"]
    tiles = _pair_tiles()

    def prep_body(i, carry):
        cs = [i * PREP_CHUNKS + j for j in range(PREP_CHUNKS)]
        rws = [pl.ds(pl.multiple_of(c * CHUNK, CHUNK), CHUNK) for c in cs]
        bl = [b_ref[pl.ds(pl.multiple_of(c * CHUNK, CHUNK) + CHUNK - 1, 1), :] for c in cs]
        ebl = [jnp.exp(b1 - b_ref[rows, :]) for b1, rows in zip(bl, rws)]
        pt_c = [pa_ref[rows, :] * e for rows, e in zip(rws, ebl)]
        kt_c = [k2_ref[rows, :] * e for rows, e in zip(rws, ebl)]
        wl = [jnp.exp(b1) for b1 in bl]
        items = [(j, p) for j in range(PREP_CHUNKS) for p in range(PAIRS)]
        qt = [qt_ref[rws[j], tiles[p]] for j, p in items]
        rt = [rt_ref[rws[j], tiles[p]] for j, p in items]
        v_t = [v_ref[rws[j], tiles[p]] for j, p in items]
        qr = [jnp.concatenate([a, b], axis=0) for a, b in zip(qt, rt)]
        aap = [_dot_nt(t, _stack(ph_ref[rws[j], tiles[p]], lo)) for (j, p), t in zip(items, qr)]
        aak = [_dot_nt(t, _stack(kh_ref[rws[j], tiles[p]], lo)) for (j, p), t in zip(items, qr)]
        av = [_dot(jnp.concatenate([jnp.where(strict, t[0:CHUNK], 0.0), jnp.where(causal, t[CHUNK:], 0.0)],
                                   axis=0), _stack(v, lo)) for t, v in zip(aak, v_t)]
        tinv = _bd_inverse_all([_stack(jnp.where(strict, t[0:CHUNK], 0.0), lo) for t in aap], eye)
        qu = [_dot(t, jnp.concatenate([_stack(a, lo), _stack(b[0:CHUNK], lo)], axis=1))
              for t, a, b in zip(tinv, qt, av)]
        t1 = [_dot(jnp.where(causal, t[CHUNK:], 0.0), y) for t, y in zip(aap, qu)]
        qu_t = [y[0:CHUNK] + y[CHUNK:] for y in qu]
        t2 = [_dot_tn(pt_c[j][:, tiles[p]], y[:, 0:LANES]) for (j, p), y in zip(items, qu_t)]
        t3 = [_dot_tn(jnp.concatenate([pt_c[j][:, tiles[p]], kt_c[j][:, tiles[p]]], axis=0),
                      jnp.concatenate([y[:, LANES:], v], axis=0))
              for (j, p), y, v in zip(items, qu_t, v_t)]
        for n, (j, p) in enumerate(items):
            qm_ref[cs[j], p, 0:CHUNK, :] = rt[n] + t1[n][:, 0:LANES]
            qm_ref[cs[j], p, CHUNK:, :] = eye * wl[j][:, tiles[p]] + jnp.where(bd, t2[n], 0.0)
            n_ref[cs[j], p] = jnp.where(bd, t3[n], 0.0)
            o_ref[rws[j], tiles[p]] = t1[n][:, LANES:] + av[n][CHUNK:]
        return carry

    lax.fori_loop(0, tb // (CHUNK * PREP_CHUNKS), prep_body, 0)

    def scan_body(c, carry):
        rows = pl.ds(pl.multiple_of(c * CHUNK, CHUNK), CHUNK)
        out = [_dot(qm_ref[c, p], s_ref[p]) for p in range(PAIRS)]
        for p in range(PAIRS):
            s_ref[p] = out[p][CHUNK:] + n_ref[c, p]
            o_ref[rows, tiles[p]] = o_ref[rows, tiles[p]] + out[p][0:CHUNK]
        return carry

    lax.fori_loop(0, tb // CHUNK, scan_body, 0)

    o = o_ref[...]
    inv_d = 1.0 / HEAD_DIM
    mean = _dot_sel(_dot_sel(o, red, SUM_PASSES) * inv_d, expd, EXPAND_PASSES)
    d = o - mean
    var = _dot_sel(d * d, red, SUM_PASSES) * inv_d
    on = d * _dot_sel(lax.rsqrt(var + RWKV_GN_EPS), expd, EXPAND_PASSES)
    on = on * lnw_ref[...] + lnb_ref[...]
    bonus = _dot_sel(_dot_sel(r * k2 * rk_ref[...], red, SUM_PASSES), expd, EXPAND_PASSES) * v
    y_ref[...] = (on + bonus) * _silu(gate)


def _hgrn_kernel(x_ref, nw_ref, w_ref, lbraw_ref, hnw_ref, cum_ref, lvl_ref, red_ref, expd_ref, y_ref,
                 s_ref, kin_ref, v_ref, b_ref, qe_ref, zl_ref, o_ref, *, layer, depth):
    tb = x_ref.shape[0]

    @pl.when(pl.program_id(0) == 0)
    def _():
        s_ref[...] = jnp.zeros_like(s_ref)

    lbs = [lbraw_ref[j:j + 1, :] for j in range(depth)]
    mx = lbs[0]
    for t in lbs[1:]:
        mx = jnp.maximum(mx, t)
    es = [jnp.exp(t - mx) for t in lbs]
    tot = es[0]
    for t in es[1:]:
        tot = tot + t
    soft = [t / tot for t in es]
    lb = soft[0]
    for t in soft[1:layer + 1]:
        lb = lb + t
    lb = lb - soft[0]

    p = _norm_proj(x_ref, nw_ref, w_ref)
    q = _silu(p[:, 0:WIDTH])
    f = p[:, WIDTH:2 * WIDTH]
    v = p[:, 2 * WIDTH:3 * WIDTH]
    gate = p[:, 3 * WIDTH:]
    log_g = jax.nn.log_sigmoid(f) + jnp.log1p(lb * jnp.exp(-f))
    kin = (1.0 - lb) * jax.nn.sigmoid(-f)

    b = _sel_dot(cum_ref[...], log_g, CUMSUM_PASSES)
    kin_ref[...] = kin
    v_ref[...] = v
    b_ref[...] = b
    qe_ref[...] = q * jnp.exp(b)
    tr = _iota((tb, WIDTH), 0)
    for l in range(LEVELS):
        d = _sel_dot(lvl_ref[l], log_g, CUMSUM_PASSES)
        zl_ref[l] = jnp.where((tr & (1 << l)) != 0, q, kin) * jnp.exp(d)

    mk = _pair_masks()
    lo, bd = mk["lo"], mk["bd"]
    kj = _iota((LANES, CHUNK), 0) & (CHUNK - 1)
    qi = _iota((LANES, CHUNK), 1)
    lvl_masks_t = [((qi >> (l + 1)) == (kj >> (l + 1))) & ((qi & (1 << l)) != 0) & ((kj & (1 << l)) == 0)
                   for l in range(LEVELS)]
    tiles = _pair_tiles()

    def chunk_body(i, carry):
        cs = [i * HGRN_CHUNKS + j for j in range(HGRN_CHUNKS)]
        rws = [pl.ds(pl.multiple_of(c * CHUNK, CHUNK), CHUNK) for c in cs]
        bl = [b_ref[pl.ds(pl.multiple_of(c * CHUNK, CHUNK) + CHUNK - 1, 1), :] for c in cs]
        kt_c = [kin_ref[rows, :] * jnp.exp(b1 - b_ref[rows, :]) for b1, rows in zip(bl, rws)]
        wl = [jnp.exp(b1) for b1 in bl]
        st = [s_ref[p] for p in range(PAIRS)]
        for j in range(HGRN_CHUNKS):
            att = [None] * PAIRS
            for l in range(LEVELS):
                for p in range(PAIRS):
                    z = zl_ref[l, rws[j], tiles[p]]
                    t = jnp.where(lvl_masks_t[l], _dot_nt(_stack(z, lo), z), 0.0)
                    att[p] = t if att[p] is None else att[p] + t
            v_t = [v_ref[rws[j], tiles[p]] for p in range(PAIRS)]
            oi = [_dot_tn(a, _stack(v, lo)) for a, v in zip(att, v_t)]
            sn = [jnp.where(bd, _dot_tn(v, kt_c[j][:, tiles[p]]), 0.0) for p, v in enumerate(v_t)]
            for p in range(PAIRS):
                o_ref[rws[j], tiles[p]] = oi[p] + _dot_nt(qe_ref[rws[j], tiles[p]], st[p])
                st[p] = st[p] * wl[j][:, tiles[p]] + sn[p]
        for p in range(PAIRS):
            s_ref[p] = st[p]
        return carry

    lax.fori_loop(0, tb // (CHUNK * HGRN_CHUNKS), chunk_body, 0)

    o = o_ref[...] + _dot_sel(_dot_sel(q * kin, red_ref[...], SUM_PASSES), expd_ref[...], EXPAND_PASSES) * v
    ms = jnp.mean(o * o, axis=-1, keepdims=True)
    y_ref[...] = (o * lax.rsqrt(ms + NORM_EPS)) * hnw_ref[...] * _silu(gate)


def _outproj_kernel(x_ref, ya_ref, yb_ref, yc_ref, w_ref, fnw_ref, o_ref, *, final):
    acc = x_ref[...]
    for i, y_ref in enumerate((ya_ref, yb_ref, yc_ref)):
        acc = acc + _dot(y_ref[...].astype(BF16), w_ref[i * WIDTH:(i + 1) * WIDTH, :])
    if final:
        ms = jnp.mean(acc * acc, axis=-1, keepdims=True)
        acc = (acc * lax.rsqrt(ms + NORM_EPS)) * fnw_ref[...]
    o_ref[...] = acc


def _full(shape):
    nd = len(shape)
    return pl.BlockSpec(shape, lambda i, _nd=nd: (0,) * _nd)


def _rows(tb, width):
    return pl.BlockSpec((tb, width), lambda i: (i, 0))


def _mixer_call(body, name, x2, params, scratch):
    t, d = x2.shape
    tb = TIME_BLOCK
    return pl.pallas_call(
        body,
        grid=(t // tb,),
        in_specs=[_rows(tb, d)] + [_full(a.shape) for a in params],
        out_specs=_rows(tb, WIDTH),
        out_shape=jax.ShapeDtypeStruct((t, WIDTH), F32),
        scratch_shapes=scratch,
        compiler_params=pltpu.CompilerParams(dimension_semantics=("arbitrary",),
                                             vmem_limit_bytes=VMEM_LIMIT),
        name=name,
    )(x2, *params)


def _slab(tb):
    return pltpu.VMEM((tb, WIDTH), F32)


def _state():
    return pltpu.VMEM((PAIRS, LANES, LANES), F32)


def _folded(nc):
    return [pltpu.VMEM((nc, PAIRS, CHUNK + LANES, LANES), F32), pltpu.VMEM((nc, PAIRS, LANES, LANES), F32)]


def _row(a):
    return a.reshape(1, -1).astype(F32)


def _lane_pad(a, offset):
    return jnp.zeros((1, LANES), F32).at[0, offset:offset + HEADS].set(a.astype(F32))


def kernel(x, norm_w, w_in, gdn_conv_w, gdn_a_log, gdn_dt_bias, gdn_norm_w, rwkv_mu, rwkv_w0, rwkv_w_up, rwkv_a0, rwkv_a_up, rwkv_k_k, rwkv_k_a, rwkv_r_k, rwkv_ln_w, rwkv_ln_b, hgrn_lower_bounds, hgrn_norm_w, w_out, final_norm_w):
    bsz, seq, d = x.shape
    depth = norm_w.shape[0]
    assert bsz == 1 and seq % TIME_BLOCK == 0 and seq % OUT_BLOCK == 0
    tb = TIME_BLOCK
    nc = tb // CHUNK
    x2 = x.reshape(seq, d)

    gdn_cols = 3 * WIDTH + 2 * HEADS + WIDTH
    rwkv_off = gdn_cols
    hgrn_off = gdn_cols + RWKV_PROJ

    cum = _const(_np_chunk_cumsum(CONST_ROWS))
    red = _const(_np_head_reduce())
    exp0 = _const(_np_head_expand(0))
    exp8 = _const(_np_head_expand(HEADS))
    lvl = _const(np.stack([_np_level_matrix(CONST_ROWS, 1 << l) for l in range(LEVELS)]))

    for l in range(depth):
        wl = w_in[l]
        nw = _row(norm_w[l])
        w_gdn = jnp.concatenate(
            [wl[:, 0:3 * WIDTH], wl[:, 3 * WIDTH + 2 * HEADS:gdn_cols],
             wl[:, 3 * WIDTH:3 * WIDTH + 2 * HEADS],
             jnp.zeros((d, LANES - 2 * HEADS), wl.dtype)], axis=1).astype(BF16)
        w_rwkv = wl[:, rwkv_off:rwkv_off + RWKV_PROJ].astype(BF16)
        w_hgrn = wl[:, hgrn_off:hgrn_off + HGRN_PROJ].astype(BF16)

        y_a = _mixer_call(
            _gdn_kernel, f"gdn_{l}", x2,
            [nw, w_gdn, gdn_conv_w[l].astype(F32), _lane_pad(gdn_a_log[l], HEADS),
             _lane_pad(gdn_dt_bias[l], HEADS), _row(jnp.tile(gdn_norm_w[l], HEADS)),
             cum, red, exp0, exp8],
            [_state(), pltpu.VMEM((tb + SUBLANES, 3 * WIDTH), F32)] + [_slab(tb)] * 7
            + [pltpu.VMEM((tb, LANES), F32)] + _folded(nc) + [_slab(tb)])

        y_b = _mixer_call(
            _rwkv_kernel, f"rwkv_{l}", x2,
            [nw, w_rwkv, _row(rwkv_mu[l]), _row(rwkv_w0[l]), rwkv_w_up[l].astype(F32),
             _row(rwkv_a0[l]), rwkv_a_up[l].astype(F32), _row(rwkv_k_k[l]), _row(rwkv_k_a[l]),
             _row(rwkv_r_k[l]), _row(rwkv_ln_w[l]), _row(rwkv_ln_b[l]), cum, red, exp0],
            [_state(), pltpu.VMEM((tb + SUBLANES, RWKV_PROJ), F32)] + [_slab(tb)] * 8
            + _folded(nc) + [_slab(tb)])

        y_c = _mixer_call(
            functools.partial(_hgrn_kernel, layer=l, depth=depth), f"hgrn_{l}", x2,
            [nw, w_hgrn, hgrn_lower_bounds.astype(F32), _row(hgrn_norm_w[l]), cum, lvl, red, exp0],
            [_state()] + [_slab(tb)] * 4 + [pltpu.VMEM((LEVELS, tb, WIDTH), F32), _slab(tb)])

        final = l == depth - 1
        x2 = pl.pallas_call(
            functools.partial(_outproj_kernel, final=final),
            grid=(seq // OUT_BLOCK,),
            in_specs=[_rows(OUT_BLOCK, d)] + [_rows(OUT_BLOCK, WIDTH)] * 3
            + [_full((3 * WIDTH, d)), _full((1, d))],
            out_specs=_rows(OUT_BLOCK, d),
            out_shape=jax.ShapeDtypeStruct((seq, d), F32),
            compiler_params=pltpu.CompilerParams(dimension_semantics=("arbitrary",),
                                                 vmem_limit_bytes=VMEM_LIMIT),
            name=f"outproj_{l}",
        )(x2, y_a, y_b, y_c, w_out[l].astype(BF16), _row(final_norm_w))

    return x2.reshape(bsz, seq, d)
```

```python
import functools

import numpy as np
import jax
import jax.numpy as jnp
from jax import lax
from jax.experimental import pallas as pl
from jax.experimental.pallas import tpu as pltpu

F32 = jnp.float32
BF16 = jnp.bfloat16

HEAD_DIM = 64
HEADS = 8
PAIRS = HEADS // 2
WIDTH = HEADS * HEAD_DIM
CHUNK = 64
LEVELS = CHUNK.bit_length() - 1
CONV_WIDTH = 4
LORA = 64
NORM_EPS = 1e-6
L2_EPS = 1e-6
RWKV_GN_EPS = 64e-5
LANES = 128
SUBLANES = 8
RWKV_PROJ = 4 * WIDTH + 2 * LORA
HGRN_PROJ = 4 * WIDTH
TIME_BLOCK = 512
CONST_ROWS = 256
PREP_CHUNKS = 4
HGRN_CHUNKS = 4
SUM_PASSES = 1
EXPAND_PASSES = 1
CUMSUM_PASSES = 2
GDN_DECAY_PASSES = 3
OUT_BLOCK = 512
VMEM_LIMIT = 56 * 1024 * 1024


def _dot(a, b):
    return jnp.dot(a, b, preferred_element_type=F32)


def _dot_nt(a, b):
    return lax.dot_general(a, b, (((1,), (1,)), ((), ())), preferred_element_type=F32)


def _dot_tn(a, b):
    return lax.dot_general(a, b, (((0,), (0,)), ((), ())), preferred_element_type=F32)


def _split_bf16(x, n):
    parts, r = [], x
    for i in range(n):
        p = r.astype(BF16)
        parts.append(p)
        if i + 1 < n:
            r = r - p.astype(F32)
    return parts


def _sel_dot(m01, x, n):
    parts = _split_bf16(x, n)
    blocks = []
    for r0 in range(0, x.shape[0], CONST_ROWS):
        out = None
        for p in parts:
            t = _dot(m01, p[r0:r0 + CONST_ROWS])
            out = t if out is None else out + t
        blocks.append(out)
    return blocks[0] if len(blocks) == 1 else jnp.concatenate(blocks, axis=0)


def _dot_sel(x, m01, n):
    out = None
    for p in _split_bf16(x, n):
        t = _dot(p, m01)
        out = t if out is None else out + t
    return out


def _iota(shape, dim):
    return lax.broadcasted_iota(jnp.int32, shape, dim)


def _bd_inverse_all(ps, eye_bd):
    xs = [eye_bd + p for p in ps]
    ps = [_dot(p, p) for p in ps]
    for _ in range(LEVELS - 2):
        r = [_dot(p, jnp.concatenate([x, p], axis=1)) for p, x in zip(ps, xs)]
        xs = [x + t[:, 0:LANES] for x, t in zip(xs, r)]
        ps = [t[:, LANES:] for t in r]
    return [x + _dot(p, x) for p, x in zip(ps, xs)]


def _pair_masks():
    i = _iota((CHUNK, LANES), 0)
    c = _iota((CHUNK, LANES), 1)
    cm = c & (HEAD_DIM - 1)
    r = _iota((LANES, LANES), 0)
    l = _iota((LANES, LANES), 1)
    return dict(lo=c < HEAD_DIM, causal=cm <= i, strict=cm < i,
                bd=(r // HEAD_DIM) == (l // HEAD_DIM), eye=jnp.where(r == l, 1.0, 0.0))


def _stack(x, lo):
    return jnp.concatenate([jnp.where(lo, x, 0.0), jnp.where(lo, 0.0, x)], axis=0)


def _pair_tiles():
    return [slice(p * LANES, (p + 1) * LANES) for p in range(PAIRS)]


def _silu(x):
    return x * jax.nn.sigmoid(x)


def _norm_proj(x_ref, nw_ref, w_ref):
    x = x_ref[...]
    ms = jnp.mean(x * x, axis=-1, keepdims=True)
    h = (x * lax.rsqrt(ms + NORM_EPS)) * nw_ref[...]
    return _dot(h.astype(BF16), w_ref[...])


def _np_head_expand(offset):
    r = np.arange(LANES)[:, None]
    c = np.arange(WIDTH)[None, :]
    return r == c // HEAD_DIM + offset


def _np_head_reduce():
    return _np_head_expand(0).T


def _np_chunk_cumsum(n):
    r = np.arange(n)[:, None]
    c = np.arange(n)[None, :]
    return (r // CHUNK == c // CHUNK) & (c <= r)


def _np_level_matrix(n, s):
    r = np.arange(n)[:, None]
    c = np.arange(n)[None, :]
    tr, tc = r % CHUNK, c % CHUNK
    mid = (tr // (2 * s)) * (2 * s) + s - 1
    upper = (tr // s) % 2 == 1
    m = (upper & (tc > mid) & (tc <= tr)) | (~upper & (tc > tr) & (tc <= mid))
    return (r // CHUNK == c // CHUNK) & m


def _const(mask):
    return jnp.asarray(np.asarray(mask, np.float32), BF16)


def _gdn_kernel(x_ref, nw_ref, w_ref, conv_ref, alog_ref, dtb_ref, gnw_ref, cum_ref, red_ref, expb_ref,
                expg_ref, y_ref,
                s_ref, ext_ref, q_ref, k_ref, kb_ref, vb_ref, kbe_ref, qd_ref, gcf_ref, gcs_ref,
                qm_ref, n_ref, o_ref):
    tb = x_ref.shape[0]

    @pl.when(pl.program_id(0) == 0)
    def _():
        s_ref[...] = jnp.zeros_like(s_ref)
        ext_ref[0:SUBLANES, :] = jnp.zeros((SUBLANES, 3 * WIDTH), F32)

    p = _norm_proj(x_ref, nw_ref, w_ref)
    gate = p[:, 3 * WIDTH:4 * WIDTH]
    small = p[:, 4 * WIDTH:4 * WIDTH + LANES]

    ext_ref[SUBLANES:SUBLANES + tb, :] = p[:, 0:3 * WIDTH]
    conv = None
    for j in range(CONV_WIDTH):
        t = conv_ref[j:j + 1, :] * ext_ref[pl.ds(SUBLANES - (CONV_WIDTH - 1) + j, tb), :]
        conv = t if conv is None else conv + t
    ext_ref[0:SUBLANES, :] = ext_ref[tb:tb + SUBLANES, :]
    qkv = _silu(conv)
    q, k, v = qkv[:, 0:WIDTH], qkv[:, WIDTH:2 * WIDTH], qkv[:, 2 * WIDTH:3 * WIDTH]

    red = red_ref[...]
    exp_b = expb_ref[...]
    q_rs = lax.rsqrt(_dot_sel(q * q, red, SUM_PASSES) + L2_EPS)
    k_rs = lax.rsqrt(_dot_sel(k * k, red, SUM_PASSES) + L2_EPS)
    qn = (q * _dot_sel(q_rs, exp_b, EXPAND_PASSES)) * (HEAD_DIM ** -0.5)
    kn = k * _dot_sel(k_rs, exp_b, EXPAND_PASSES)

    beta = _dot_sel(jax.nn.sigmoid(small), exp_b, EXPAND_PASSES)
    g_small = -jnp.exp(alog_ref[...]) * jax.nn.softplus(small + dtb_ref[...])
    gc_small = _sel_dot(cum_ref[...], g_small, GDN_DECAY_PASSES)
    gc_full = _dot_sel(gc_small, expg_ref[...], GDN_DECAY_PASSES)
    egc = jnp.exp(gc_full)
    kb = kn * beta

    q_ref[...] = qn
    k_ref[...] = kn
    kb_ref[...] = kb
    vb_ref[...] = v * beta
    kbe_ref[...] = kb * egc
    qd_ref[...] = qn * egc
    gcf_ref[...] = gc_full
    gcs_ref[...] = gc_small

    mk = _pair_masks()
    lo, causal, strict, bd, eye = mk["lo"], mk["causal"], mk["strict"], mk["bd"], mk["eye"]
    tiles = _pair_tiles()

    def prep_body(i, carry):
        cs = [i * PREP_CHUNKS + j for j in range(PREP_CHUNKS)]
        rws = [pl.ds(pl.multiple_of(c * CHUNK, CHUNK), CHUNK) for c in cs]
        gc_c = [gcf_ref[rows, :] for rows in rws]
        gl = [gcf_ref[pl.ds(pl.multiple_of(c * CHUNK, CHUNK) + CHUNK - 1, 1), :] for c in cs]
        kd_c = [k_ref[rows, :] * jnp.exp(g1 - g2) for rows, g1, g2 in zip(rws, gl, gc_c)]
        last = [jnp.exp(g1) for g1 in gl]
        g_rows = [gcs_ref[rows, :].T for rows in rws]
        items = [(j, p) for j in range(PREP_CHUNKS) for p in range(PAIRS)]
        ak = [_dot_nt(jnp.concatenate([kb_ref[rws[j], tiles[p]], q_ref[rws[j], tiles[p]]], axis=0),
                      _stack(k_ref[rws[j], tiles[p]], lo)) for j, p in items]
        g_row = [jnp.concatenate([g_rows[j][HEADS + 2 * p:HEADS + 2 * p + 1, :],
                                  g_rows[j][HEADS + 2 * p + 1:HEADS + 2 * p + 2, :]], axis=1)
                 for j, p in items]
        dm = [jnp.where(causal, jnp.exp(jnp.where(causal, gc_c[j][:, tiles[p]] - g, 0.0)), 0.0)
              for (j, p), g in zip(items, g_row)]
        qk = [t[CHUNK:] * d for t, d in zip(ak, dm)]
        tm = _bd_inverse_all([_stack(jnp.where(strict, -(t[0:CHUNK] * d), 0.0), lo)
                              for t, d in zip(ak, dm)], eye)
        uw = [_dot(t, jnp.concatenate([_stack(vb_ref[rws[j], tiles[p]], lo),
                                       _stack(kbe_ref[rws[j], tiles[p]], lo)], axis=1))
              for (j, p), t in zip(items, tm)]
        t1 = [_dot(a, b) for a, b in zip(qk, uw)]
        t2 = [_dot_tn(kd_c[j][:, tiles[p]], b[0:CHUNK] + b[CHUNK:])
              for (j, p), b in zip(items, uw)]
        for n, (j, p) in enumerate(items):
            qm_ref[cs[j], p, 0:CHUNK, :] = qd_ref[rws[j], tiles[p]] - t1[n][:, LANES:]
            qm_ref[cs[j], p, CHUNK:, :] = eye * last[j][:, tiles[p]] - jnp.where(bd, t2[n][:, LANES:], 0.0)
            n_ref[cs[j], p] = jnp.where(bd, t2[n][:, 0:LANES], 0.0)
            o_ref[rws[j], tiles[p]] = t1[n][:, 0:LANES]
        return carry

    lax.fori_loop(0, tb // (CHUNK * PREP_CHUNKS), prep_body, 0)

    def scan_body(c, carry):
        rows = pl.ds(pl.multiple_of(c * CHUNK, CHUNK), CHUNK)
        out = [_dot(qm_ref[c, p], s_ref[p]) for p in range(PAIRS)]
        for p in range(PAIRS):
            s_ref[p] = out[p][CHUNK:] + n_ref[c, p]
            o_ref[rows, tiles[p]] = o_ref[rows, tiles[p]] + out[p][0:CHUNK]
        return carry

    lax.fori_loop(0, tb // CHUNK, scan_body, 0)

    o = o_ref[...]
    rs = lax.rsqrt(_dot_sel(o * o, red_ref[...], SUM_PASSES) * (1.0 / HEAD_DIM) + NORM_EPS)
    y_ref[...] = (o * _dot_sel(rs, expb_ref[...], EXPAND_PASSES)) * gnw_ref[...] * _silu(gate)


def _rwkv_kernel(x_ref, nw_ref, w_ref, mu_ref, w0_ref, wup_ref, a0_ref, aup_ref, kk_ref, ka_ref,
                 rk_ref, lnw_ref, lnb_ref, cum_ref, red_ref, expd_ref, y_ref,
                 s_ref, ext_ref, qt_ref, rt_ref, ph_ref, kh_ref, pa_ref, k2_ref, v_ref, b_ref,
                 qm_ref, n_ref, o_ref):
    tb = x_ref.shape[0]

    @pl.when(pl.program_id(0) == 0)
    def _():
        s_ref[...] = jnp.zeros_like(s_ref)
        ext_ref[0:SUBLANES, :] = jnp.zeros((SUBLANES, RWKV_PROJ), F32)

    p = _norm_proj(x_ref, nw_ref, w_ref)
    ext_ref[SUBLANES:SUBLANES + tb, :] = p
    prev = ext_ref[pl.ds(SUBLANES - 1, tb), :]
    ext_ref[0:SUBLANES, :] = ext_ref[tb:tb + SUBLANES, :]
    p = p + mu_ref[...] * (prev - p)
    r, k, v = p[:, 0:WIDTH], p[:, WIDTH:2 * WIDTH], p[:, 2 * WIDTH:3 * WIDTH]
    wd = p[:, 3 * WIDTH:3 * WIDTH + LORA]
    ad = p[:, 3 * WIDTH + LORA:3 * WIDTH + 2 * LORA]
    gate = p[:, 3 * WIDTH + 2 * LORA:]

    w_raw = -jax.nn.softplus(-(w0_ref[...] + _dot(jnp.tanh(wd), wup_ref[...]))) - 0.5
    logw = -jnp.exp(w_raw)
    a = jax.nn.sigmoid(a0_ref[...] + _dot(ad, aup_ref[...]))

    red = red_ref[...]
    expd = expd_ref[...]
    kkp = k * kk_ref[...]
    kk = kkp * _dot_sel(lax.rsqrt(_dot_sel(kkp * kkp, red, SUM_PASSES) + L2_EPS), expd, EXPAND_PASSES)
    k2 = k * (1.0 + (a - 1.0) * ka_ref[...])
    pa = kk * a

    b = _sel_dot(cum_ref[...], logw, CUMSUM_PASSES)
    enb = jnp.exp(-b)
    qt_ref[...] = -kk * jnp.exp(b - logw)
    rt_ref[...] = r * jnp.exp(b)
    ph_ref[...] = pa * enb
    kh_ref[...] = k2 * enb
    pa_ref[...] = pa
    k2_ref[...] = k2
    v_ref[...] = v
    b_ref[...] = b

    mk = _pair_masks()
    lo, causal, strict, bd, eye = mk["lo"], mk["causal"], mk["strict"], mk["bd"], mk["eye"]
    tiles = _pair_tiles()

    def prep_body(i, carry):
        cs = [i * PREP_CHUNKS + j for j in range(PREP_CHUNKS)]
        rws = [pl.ds(pl.multiple_of(c * CHUNK, CHUNK), CHUNK) for c in cs]
        bl = [b_ref[pl.ds(pl.multiple_of(c * CHUNK, CHUNK) + CHUNK - 1, 1), :] for c in cs]
        ebl = [jnp.exp(b1 - b_ref[rows, :]) for b1, rows in zip(bl, rws)]
        pt_c = [pa_ref[rows, :] * e for rows, e in zip(rws, ebl)]
        kt_c = [k2_ref[rows, :] * e for rows, e in zip(rws, ebl)]
        wl = [jnp.exp(b1) for b1 in bl]
        items = [(j, p) for j in range(PREP_CHUNKS) for p in range(PAIRS)]
        qt = [qt_ref[rws[j], tiles[p]] for j, p in items]
        rt = [rt_ref[rws[j], tiles[p]] for j, p in items]
        v_t = [v_ref[rws[j], tiles[p]] for j, p in items]
        qr = [jnp.concatenate([a, b], axis=0) for a, b in zip(qt, rt)]
        aap = [_dot_nt(t, _stack(ph_ref[rws[j], tiles[p]], lo)) for (j, p), t in zip(items, qr)]
        aak = [_dot_nt(t, _stack(kh_ref[rws[j], tiles[p]], lo)) for (j, p), t in zip(items, qr)]
        av = [_dot(jnp.concatenate([jnp.where(strict, t[0:CHUNK], 0.0), jnp.where(causal, t[CHUNK:], 0.0)],
                                   axis=0), _stack(v, lo)) for t, v in zip(aak, v_t)]
        tinv = _bd_inverse_all([_stack(jnp.where(strict, t[0:CHUNK], 0.0), lo) for t in aap], eye)
        qu = [_dot(t, jnp.concatenate([_stack(a, lo), _stack(b[0:CHUNK], lo)], axis=1))
              for t, a, b in zip(tinv, qt, av)]
        t1 = [_dot(jnp.where(causal, t[CHUNK:], 0.0), y) for t, y in zip(aap, qu)]
        qu_t = [y[0:CHUNK] + y[CHUNK:] for y in qu]
        t2 = [_dot_tn(pt_c[j][:, tiles[p]], y[:, 0:LANES]) for (j, p), y in zip(items, qu_t)]
        t3 = [_dot_tn(jnp.concatenate([pt_c[j][:, tiles[p]], kt_c[j][:, tiles[p]]], axis=0),
                      jnp.concatenate([y[:, LANES:], v], axis=0))
              for (j, p), y, v in zip(items, qu_t, v_t)]
        for n, (j, p) in enumerate(items):
            qm_ref[cs[j], p, 0:CHUNK, :] = rt[n] + t1[n][:, 0:LANES]
            qm_ref[cs[j], p, CHUNK:, :] = eye * wl[j][:, tiles[p]] + jnp.where(bd, t2[n], 0.0)
            n_ref[cs[j], p] = jnp.where(bd, t3[n], 0.0)
            o_ref[rws[j], tiles[p]] = t1[n][:, LANES:] + av[n][CHUNK:]
        return carry

    lax.fori_loop(0, tb // (CHUNK * PREP_CHUNKS), prep_body, 0)

    def scan_body(c, carry):
        rows = pl.ds(pl.multiple_of(c * CHUNK, CHUNK), CHUNK)
        out = [_dot(qm_ref[c, p], s_ref[p]) for p in range(PAIRS)]
        for p in range(PAIRS):
            s_ref[p] = out[p][CHUNK:] + n_ref[c, p]
            o_ref[rows, tiles[p]] = o_ref[rows, tiles[p]] + out[p][0:CHUNK]
        return carry

    lax.fori_loop(0, tb // CHUNK, scan_body, 0)

    o = o_ref[...]
    inv_d = 1.0 / HEAD_DIM
    mean = _dot_sel(_dot_sel(o, red, SUM_PASSES) * inv_d, expd, EXPAND_PASSES)
    d = o - mean
    var = _dot_sel(d * d, red, SUM_PASSES) * inv_d
    on = d * _dot_sel(lax.rsqrt(var + RWKV_GN_EPS), expd, EXPAND_PASSES)
    on = on * lnw_ref[...] + lnb_ref[...]
    bonus = _dot_sel(_dot_sel(r * k2 * rk_ref[...], red, SUM_PASSES), expd, EXPAND_PASSES) * v
    y_ref[...] = (on + bonus) * _silu(gate)


def _hgrn_kernel(x_ref, nw_ref, w_ref, lbraw_ref, hnw_ref, cum_ref, lvl_ref, red_ref, expd_ref, y_ref,
                 s_ref, kin_ref, v_ref, b_ref, qe_ref, zl_ref, o_ref, *, layer, depth):
    tb = x_ref.shape[0]

    @pl.when(pl.program_id(0) == 0)
    def _():
        s_ref[...] = jnp.zeros_like(s_ref)

    lbs = [lbraw_ref[j:j + 1, :] for j in range(depth)]
    mx = lbs[0]
    for t in lbs[1:]:
        mx = jnp.maximum(mx, t)
    es = [jnp.exp(t - mx) for t in lbs]
    tot = es[0]
    for t in es[1:]:
        tot = tot + t
    soft = [t / tot for t in es]
    lb = soft[0]
    for t in soft[1:layer + 1]:
        lb = lb + t
    lb = lb - soft[0]

    p = _norm_proj(x_ref, nw_ref, w_ref)
    q = _silu(p[:, 0:WIDTH])
    f = p[:, WIDTH:2 * WIDTH]
    v = p[:, 2 * WIDTH:3 * WIDTH]
    gate = p[:, 3 * WIDTH:]
    log_g = jax.nn.log_sigmoid(f) + jnp.log1p(lb * jnp.exp(-f))
    kin = (1.0 - lb) * jax.nn.sigmoid(-f)

    b = _sel_dot(cum_ref[...], log_g, CUMSUM_PASSES)
    kin_ref[...] = kin
    v_ref[...] = v
    b_ref[...] = b
    qe_ref[...] = q * jnp.exp(b)
    tr = _iota((tb, WIDTH), 0)
    for l in range(LEVELS):
        d = _sel_dot(lvl_ref[l], log_g, CUMSUM_PASSES)
        zl_ref[l] = jnp.where((tr & (1 << l)) != 0, q, kin) * jnp.exp(d)

    mk = _pair_masks()
    lo, bd = mk["lo"], mk["bd"]
    kj = _iota((LANES, CHUNK), 0) & (CHUNK - 1)
    qi = _iota((LANES, CHUNK), 1)
    lvl_masks_t = [((qi >> (l + 1)) == (kj >> (l + 1))) & ((qi & (1 << l)) != 0) & ((kj & (1 << l)) == 0)
                   for l in range(LEVELS)]
    tiles = _pair_tiles()

    def chunk_body(i, carry):
        cs = [i * HGRN_CHUNKS + j for j in range(HGRN_CHUNKS)]
        rws = [pl.ds(pl.multiple_of(c * CHUNK, CHUNK), CHUNK) for c in cs]
        bl = [b_ref[pl.ds(pl.multiple_of(c * CHUNK, CHUNK) + CHUNK - 1, 1), :] for c in cs]
        kt_c = [kin_ref[rows, :] * jnp.exp(b1 - b_ref[rows, :]) for b1, rows in zip(bl, rws)]
        wl = [jnp.exp(b1) for b1 in bl]
        st = [s_ref[p] for p in range(PAIRS)]
        for j in range(HGRN_CHUNKS):
            att = [None] * PAIRS
            for l in range(LEVELS):
                for p in range(PAIRS):
                    z = zl_ref[l, rws[j], tiles[p]]
                    t = jnp.where(lvl_masks_t[l], _dot_nt(_stack(z, lo), z), 0.0)
                    att[p] = t if att[p] is None else att[p] + t
            v_t = [v_ref[rws[j], tiles[p]] for p in range(PAIRS)]
            oi = [_dot_tn(a, _stack(v, lo)) for a, v in zip(att, v_t)]
            sn = [jnp.where(bd, _dot_tn(v, kt_c[j][:, tiles[p]]), 0.0) for p, v in enumerate(v_t)]
            for p in range(PAIRS):
                o_ref[rws[j], tiles[p]] = oi[p] + _dot_nt(qe_ref[rws[j], tiles[p]], st[p])
                st[p] = st[p] * wl[j][:, tiles[p]] + sn[p]
        for p in range(PAIRS):
            s_ref[p] = st[p]
        return carry

    lax.fori_loop(0, tb // (CHUNK * HGRN_CHUNKS), chunk_body, 0)

    o = o_ref[...] + _dot_sel(_dot_sel(q * kin, red_ref[...], SUM_PASSES), expd_ref[...], EXPAND_PASSES) * v
    ms = jnp.mean(o * o, axis=-1, keepdims=True)
    y_ref[...] = (o * lax.rsqrt(ms + NORM_EPS)) * hnw_ref[...] * _silu(gate)


def _outproj_kernel(x_ref, ya_ref, yb_ref, yc_ref, w_ref, fnw_ref, o_ref, *, final):
    acc = x_ref[...]
    for i, y_ref in enumerate((ya_ref, yb_ref, yc_ref)):
        acc = acc + _dot(y_ref[...].astype(BF16), w_ref[i * WIDTH:(i + 1) * WIDTH, :])
    if final:
        ms = jnp.mean(acc * acc, axis=-1, keepdims=True)
        acc = (acc * lax.rsqrt(ms + NORM_EPS)) * fnw_ref[...]
    o_ref[...] = acc


def _full(shape):
    nd = len(shape)
    return pl.BlockSpec(shape, lambda i, _nd=nd: (0,) * _nd)


def _rows(tb, width):
    return pl.BlockSpec((tb, width), lambda i: (i, 0))


def _mixer_call(body, name, x2, params, scratch):
    t, d = x2.shape
    tb = TIME_BLOCK
    return pl.pallas_call(
        body,
        grid=(t // tb,),
        in_specs=[_rows(tb, d)] + [_full(a.shape) for a in params],
        out_specs=_rows(tb, WIDTH),
        out_shape=jax.ShapeDtypeStruct((t, WIDTH), F32),
        scratch_shapes=scratch,
        compiler_params=pltpu.CompilerParams(dimension_semantics=("arbitrary",),
                                             vmem_limit_bytes=VMEM_LIMIT),
        name=name,
    )(x2, *params)


def _slab(tb):
    return pltpu.VMEM((tb, WIDTH), F32)


def _state():
    return pltpu.VMEM((PAIRS, LANES, LANES), F32)


def _folded(nc):
    return [pltpu.VMEM((nc, PAIRS, CHUNK + LANES, LANES), F32), pltpu.VMEM((nc, PAIRS, LANES, LANES), F32)]


def _row(a):
    return a.reshape(1, -1).astype(F32)


def _lane_pad(a, offset):
    return jnp.zeros((1, LANES), F32).at[0, offset:offset + HEADS].set(a.astype(F32))


def kernel(x, norm_w, w_in, gdn_conv_w, gdn_a_log, gdn_dt_bias, gdn_norm_w, rwkv_mu, rwkv_w0, rwkv_w_up, rwkv_a0, rwkv_a_up, rwkv_k_k, rwkv_k_a, rwkv_r_k, rwkv_ln_w, rwkv_ln_b, hgrn_lower_bounds, hgrn_norm_w, w_out, final_norm_w):
    bsz, seq, d = x.shape
    depth = norm_w.shape[0]
    assert bsz == 1 and seq % TIME_BLOCK == 0 and seq % OUT_BLOCK == 0
    tb = TIME_BLOCK
    nc = tb // CHUNK
    x2 = x.reshape(seq, d)

    gdn_cols = 3 * WIDTH + 2 * HEADS + WIDTH
    rwkv_off = gdn_cols
    hgrn_off = gdn_cols + RWKV_PROJ

    cum = _const(_np_chunk_cumsum(CONST_ROWS))
    red = _const(_np_head_reduce())
    exp0 = _const(_np_head_expand(0))
    exp8 = _const(_np_head_expand(HEADS))
    lvl = _const(np.stack([_np_level_matrix(CONST_ROWS, 1 << l) for l in range(LEVELS)]))

    for l in range(depth):
        wl = w_in[l]
        nw = _row(norm_w[l])
        w_gdn = jnp.concatenate(
            [wl[:, 0:3 * WIDTH], wl[:, 3 * WIDTH + 2 * HEADS:gdn_cols],
             wl[:, 3 * WIDTH:3 * WIDTH + 2 * HEADS],
             jnp.zeros((d, LANES - 2 * HEADS), wl.dtype)], axis=1).astype(BF16)
        w_rwkv = wl[:, rwkv_off:rwkv_off + RWKV_PROJ].astype(BF16)
        w_hgrn = wl[:, hgrn_off:hgrn_off + HGRN_PROJ].astype(BF16)

        y_a = _mixer_call(
            _gdn_kernel, f"gdn_{l}", x2,
            [nw, w_gdn, gdn_conv_w[l].astype(F32), _lane_pad(gdn_a_log[l], HEADS),
             _lane_pad(gdn_dt_bias[l], HEADS), _row(jnp.tile(gdn_norm_w[l], HEADS)),
             cum, red, exp0, exp8],
            [_state(), pltpu.VMEM((tb + SUBLANES, 3 * WIDTH), F32)] + [_slab(tb)] * 7
            + [pltpu.VMEM((tb, LANES), F32)] + _folded(nc) + [_slab(tb)])

        y_b = _mixer_call(
            _rwkv_kernel, f"rwkv_{l}", x2,
            [nw, w_rwkv, _row(rwkv_mu[l]), _row(rwkv_w0[l]), rwkv_w_up[l].astype(F32),
             _row(rwkv_a0[l]), rwkv_a_up[l].astype(F32), _row(rwkv_k_k[l]), _row(rwkv_k_a[l]),
             _row(rwkv_r_k[l]), _row(rwkv_ln_w[l]), _row(rwkv_ln_b[l]), cum, red, exp0],
            [_state(), pltpu.VMEM((tb + SUBLANES, RWKV_PROJ), F32)] + [_slab(tb)] * 8
            + _folded(nc) + [_slab(tb)])

        y_c = _mixer_call(
            functools.partial(_hgrn_kernel, layer=l, depth=depth), f"hgrn_{l}", x2,
            [nw, w_hgrn, hgrn_lower_bounds.astype(F32), _row(hgrn_norm_w[l]), cum, lvl, red, exp0],
            [_state()] + [_slab(tb)] * 4 + [pltpu.VMEM((LEVELS, tb, WIDTH), F32), _slab(tb)])

        final = l == depth - 1
        x2 = pl.pallas_call(
            functools.partial(_outproj_kernel, final=final),
            grid=(seq // OUT_BLOCK,),
            in_specs=[_rows(OUT_BLOCK, d)] + [_rows(OUT_BLOCK, WIDTH)] * 3
            + [_full((3 * WIDTH, d)), _full((1, d))],
            out_specs=_rows(OUT_BLOCK, d),
            out_shape=jax.ShapeDtypeStruct((seq, d), F32),
            compiler_params=pltpu.CompilerParams(dimension_semantics=("arbitrary",),
                                                 vmem_limit_bytes=VMEM_LIMIT),
            name=f"outproj_{l}",
        )(x2, y_a, y_b, y_c, w_out[l].astype(BF16), _row(final_norm_w))

    return x2.reshape(bsz, seq, d)
```

```python
import functools

import numpy as np
import jax
import jax.numpy as jnp
from jax import lax
from jax.experimental import pallas as pl
from jax.experimental.pallas import tpu as pltpu

F32 = jnp.float32
BF16 = jnp.bfloat16

HEAD_DIM = 64
HEADS = 8
PAIRS = HEADS // 2
WIDTH = HEADS * HEAD_DIM
CHUNK = 64
LEVELS = CHUNK.bit_length() - 1
CONV_WIDTH = 4
LORA = 64
NORM_EPS = 1e-6
L2_EPS = 1e-6
RWKV_GN_EPS = 64e-5
LANES = 128
SUBLANES = 8
MATMUL_LEVELS = [l for l in range(1, LEVELS) if 2 * (1 << l) < SUBLANES]
RWKV_PROJ = 4 * WIDTH + 2 * LORA
HGRN_PROJ = 4 * WIDTH
TIME_BLOCK = 512
CONST_ROWS = 256
PREP_CHUNKS = 4
HGRN_CHUNKS = 4
SUM_PASSES = 1
EXPAND_PASSES = 1
CUMSUM_PASSES = 2
GDN_DECAY_PASSES = 3
OUT_BLOCK = 512
VMEM_LIMIT = 56 * 1024 * 1024


def _dot(a, b):
    return jnp.dot(a, b, preferred_element_type=F32)


def _dot_nt(a, b):
    return lax.dot_general(a, b, (((1,), (1,)), ((), ())), preferred_element_type=F32)


def _dot_tn(a, b):
    return lax.dot_general(a, b, (((0,), (0,)), ((), ())), preferred_element_type=F32)


def _split_bf16(x, n):
    parts, r = [], x
    for i in range(n):
        p = r.astype(BF16)
        parts.append(p)
        if i + 1 < n:
            r = r - p.astype(F32)
    return parts


def _sel_dot(m01, x, n):
    parts = _split_bf16(x, n)
    blocks = []
    for r0 in range(0, x.shape[0], CONST_ROWS):
        out = None
        for p in parts:
            t = _dot(m01, p[r0:r0 + CONST_ROWS])
            out = t if out is None else out + t
        blocks.append(out)
    return blocks[0] if len(blocks) == 1 else jnp.concatenate(blocks, axis=0)


def _dot_sel(x, m01, n):
    out = None
    for p in _split_bf16(x, n):
        t = _dot(p, m01)
        out = t if out is None else out + t
    return out


def _iota(shape, dim):
    return lax.broadcasted_iota(jnp.int32, shape, dim)


def _bd_inverse_all(ps, eye_bd):
    xs = [eye_bd + p for p in ps]
    ps = [_dot(p, p) for p in ps]
    for _ in range(LEVELS - 2):
        r = [_dot(p, jnp.concatenate([x, p], axis=1)) for p, x in zip(ps, xs)]
        xs = [x + t[:, 0:LANES] for x, t in zip(xs, r)]
        ps = [t[:, LANES:] for t in r]
    return [x + _dot(p, x) for p, x in zip(ps, xs)]


def _pair_masks():
    i = _iota((CHUNK, LANES), 0)
    c = _iota((CHUNK, LANES), 1)
    cm = c & (HEAD_DIM - 1)
    r = _iota((LANES, LANES), 0)
    l = _iota((LANES, LANES), 1)
    return dict(lo=c < HEAD_DIM, causal=cm <= i, strict=cm < i,
                bd=(r // HEAD_DIM) == (l // HEAD_DIM), eye=jnp.where(r == l, 1.0, 0.0))


def _stack(x, lo):
    return jnp.concatenate([jnp.where(lo, x, 0.0), jnp.where(lo, 0.0, x)], axis=0)


def _pair_tiles():
    return [slice(p * LANES, (p + 1) * LANES) for p in range(PAIRS)]


def _silu(x):
    return x * jax.nn.sigmoid(x)


def _norm_proj(x_ref, nw_ref, w_ref):
    x = x_ref[...]
    ms = jnp.mean(x * x, axis=-1, keepdims=True)
    h = (x * lax.rsqrt(ms + NORM_EPS)) * nw_ref[...]
    return _dot(h.astype(BF16), w_ref[...])


def _np_head_expand(offset):
    r = np.arange(LANES)[:, None]
    c = np.arange(WIDTH)[None, :]
    return r == c // HEAD_DIM + offset


def _np_head_reduce():
    return _np_head_expand(0).T


def _np_chunk_cumsum(n):
    r = np.arange(n)[:, None]
    c = np.arange(n)[None, :]
    return (r // CHUNK == c // CHUNK) & (c <= r)


def _np_level_matrix(n, s):
    r = np.arange(n)[:, None]
    c = np.arange(n)[None, :]
    tr, tc = r % CHUNK, c % CHUNK
    mid = (tr // (2 * s)) * (2 * s) + s - 1
    upper = (tr // s) % 2 == 1
    m = (upper & (tc > mid) & (tc <= tr)) | (~upper & (tc > tr) & (tc <= mid))
    return (r // CHUNK == c // CHUNK) & m


def _const(mask):
    return jnp.asarray(np.asarray(mask, np.float32), BF16)


def _gdn_kernel(x_ref, nw_ref, w_ref, conv_ref, alog_ref, dtb_ref, gnw_ref, cum_ref, red_ref, expb_ref,
                expg_ref, y_ref,
                s_ref, ext_ref, q_ref, k_ref, kb_ref, vb_ref, kbe_ref, qd_ref, gcf_ref, gcs_ref,
                qm_ref, n_ref, o_ref):
    tb = x_ref.shape[0]

    @pl.when(pl.program_id(0) == 0)
    def _():
        s_ref[...] = jnp.zeros_like(s_ref)
        ext_ref[0:SUBLANES, :] = jnp.zeros((SUBLANES, 3 * WIDTH), F32)

    p = _norm_proj(x_ref, nw_ref, w_ref)
    gate = p[:, 3 * WIDTH:4 * WIDTH]
    small = p[:, 4 * WIDTH:4 * WIDTH + LANES]

    ext_ref[SUBLANES:SUBLANES + tb, :] = p[:, 0:3 * WIDTH]
    conv = None
    for j in range(CONV_WIDTH):
        t = conv_ref[j:j + 1, :] * ext_ref[pl.ds(SUBLANES - (CONV_WIDTH - 1) + j, tb), :]
        conv = t if conv is None else conv + t
    ext_ref[0:SUBLANES, :] = ext_ref[tb:tb + SUBLANES, :]
    qkv = _silu(conv)
    q, k, v = qkv[:, 0:WIDTH], qkv[:, WIDTH:2 * WIDTH], qkv[:, 2 * WIDTH:3 * WIDTH]

    red = red_ref[...]
    exp_b = expb_ref[...]
    q_rs = lax.rsqrt(_dot_sel(q * q, red, SUM_PASSES) + L2_EPS)
    k_rs = lax.rsqrt(_dot_sel(k * k, red, SUM_PASSES) + L2_EPS)
    qn = (q * _dot_sel(q_rs, exp_b, EXPAND_PASSES)) * (HEAD_DIM ** -0.5)
    kn = k * _dot_sel(k_rs, exp_b, EXPAND_PASSES)

    beta = _dot_sel(jax.nn.sigmoid(small), exp_b, EXPAND_PASSES)
    g_small = -jnp.exp(alog_ref[...]) * jax.nn.softplus(small + dtb_ref[...])
    gc_small = _sel_dot(cum_ref[...], g_small, GDN_DECAY_PASSES)
    gc_full = _dot_sel(gc_small, expg_ref[...], GDN_DECAY_PASSES)
    egc = jnp.exp(gc_full)
    kb = kn * beta

    q_ref[...] = qn
    k_ref[...] = kn
    kb_ref[...] = kb
    vb_ref[...] = v * beta
    kbe_ref[...] = kb * egc
    qd_ref[...] = qn * egc
    gcf_ref[...] = gc_full
    gcs_ref[...] = gc_small

    mk = _pair_masks()
    lo, causal, strict, bd, eye = mk["lo"], mk["causal"], mk["strict"], mk["bd"], mk["eye"]
    tiles = _pair_tiles()

    def prep_body(i, carry):
        cs = [i * PREP_CHUNKS + j for j in range(PREP_CHUNKS)]
        rws = [pl.ds(pl.multiple_of(c * CHUNK, CHUNK), CHUNK) for c in cs]
        gc_c = [gcf_ref[rows, :] for rows in rws]
        gl = [gcf_ref[pl.ds(pl.multiple_of(c * CHUNK, CHUNK) + CHUNK - 1, 1), :] for c in cs]
        kd_c = [k_ref[rows, :] * jnp.exp(g1 - g2) for rows, g1, g2 in zip(rws, gl, gc_c)]
        last = [jnp.exp(g1) for g1 in gl]
        g_rows = [gcs_ref[rows, :].T for rows in rws]
        items = [(j, p) for j in range(PREP_CHUNKS) for p in range(PAIRS)]
        ak = [_dot_nt(jnp.concatenate([kb_ref[rws[j], tiles[p]], q_ref[rws[j], tiles[p]]], axis=0),
                      _stack(k_ref[rws[j], tiles[p]], lo)) for j, p in items]
        g_row = [jnp.concatenate([g_rows[j][HEADS + 2 * p:HEADS + 2 * p + 1, :],
                                  g_rows[j][HEADS + 2 * p + 1:HEADS + 2 * p + 2, :]], axis=1)
                 for j, p in items]
        dm = [jnp.where(causal, jnp.exp(jnp.where(causal, gc_c[j][:, tiles[p]] - g, 0.0)), 0.0)
              for (j, p), g in zip(items, g_row)]
        qk = [t[CHUNK:] * d for t, d in zip(ak, dm)]
        tm = _bd_inverse_all([_stack(jnp.where(strict, -(t[0:CHUNK] * d), 0.0), lo)
                              for t, d in zip(ak, dm)], eye)
        uw = [_dot(t, jnp.concatenate([_stack(vb_ref[rws[j], tiles[p]], lo),
                                       _stack(kbe_ref[rws[j], tiles[p]], lo)], axis=1))
              for (j, p), t in zip(items, tm)]
        t1 = [_dot(a, b) for a, b in zip(qk, uw)]
        t2 = [_dot_tn(kd_c[j][:, tiles[p]], b[0:CHUNK] + b[CHUNK:])
              for (j, p), b in zip(items, uw)]
        for n, (j, p) in enumerate(items):
            qm_ref[cs[j], p, 0:CHUNK, :] = qd_ref[rws[j], tiles[p]] - t1[n][:, LANES:]
            qm_ref[cs[j], p, CHUNK:, :] = eye * last[j][:, tiles[p]] - jnp.where(bd, t2[n][:, LANES:], 0.0)
            n_ref[cs[j], p] = jnp.where(bd, t2[n][:, 0:LANES], 0.0)
            o_ref[rws[j], tiles[p]] = t1[n][:, 0:LANES]
        return carry

    lax.fori_loop(0, tb // (CHUNK * PREP_CHUNKS), prep_body, 0)

    def scan_body(c, carry):
        rows = pl.ds(pl.multiple_of(c * CHUNK, CHUNK), CHUNK)
        out = [_dot(qm_ref[c, p], s_ref[p]) for p in range(PAIRS)]
        for p in range(PAIRS):
            s_ref[p] = out[p][CHUNK:] + n_ref[c, p]
            o_ref[rows, tiles[p]] = o_ref[rows, tiles[p]] + out[p][0:CHUNK]
        return carry

    lax.fori_loop(0, tb // CHUNK, scan_body, 0)

    o = o_ref[...]
    rs = lax.rsqrt(_dot_sel(o * o, red_ref[...], SUM_PASSES) * (1.0 / HEAD_DIM) + NORM_EPS)
    y_ref[...] = (o * _dot_sel(rs, expb_ref[...], EXPAND_PASSES)) * gnw_ref[...] * _silu(gate)


def _rwkv_kernel(x_ref, nw_ref, w_ref, mu_ref, w0_ref, wup_ref, a0_ref, aup_ref, kk_ref, ka_ref,
                 rk_ref, lnw_ref, lnb_ref, cum_ref, red_ref, expd_ref, y_ref,
                 s_ref, ext_ref, qt_ref, rt_ref, ph_ref, kh_ref, pa_ref, k2_ref, v_ref, b_ref,
                 qm_ref, n_ref, o_ref):
    tb = x_ref.shape[0]

    @pl.when(pl.program_id(0) == 0)
    def _():
        s_ref[...] = jnp.zeros_like(s_ref)
        ext_ref[0:SUBLANES, :] = jnp.zeros((SUBLANES, RWKV_PROJ), F32)

    p = _norm_proj(x_ref, nw_ref, w_ref)
    ext_ref[SUBLANES:SUBLANES + tb, :] = p
    prev = ext_ref[pl.ds(SUBLANES - 1, tb), :]
    ext_ref[0:SUBLANES, :] = ext_ref[tb:tb + SUBLANES, :]
    p = p + mu_ref[...] * (prev - p)
    r, k, v = p[:, 0:WIDTH], p[:, WIDTH:2 * WIDTH], p[:, 2 * WIDTH:3 * WIDTH]
    wd = p[:, 3 * WIDTH:3 * WIDTH + LORA]
    ad = p[:, 3 * WIDTH + LORA:3 * WIDTH + 2 * LORA]
    gate = p[:, 3 * WIDTH + 2 * LORA:]

    w_raw = -jax.nn.softplus(-(w0_ref[...] + _dot(jnp.tanh(wd), wup_ref[...]))) - 0.5
    logw = -jnp.exp(w_raw)
    a = jax.nn.sigmoid(a0_ref[...] + _dot(ad, aup_ref[...]))

    red = red_ref[...]
    expd = expd_ref[...]
    kkp = k * kk_ref[...]
    kk = kkp * _dot_sel(lax.rsqrt(_dot_sel(kkp * kkp, red, SUM_PASSES) + L2_EPS), expd, EXPAND_PASSES)
    k2 = k * (1.0 + (a - 1.0) * ka_ref[...])
    pa = kk * a

    b = _sel_dot(cum_ref[...], logw, CUMSUM_PASSES)
    enb = jnp.exp(-b)
    qt_ref[...] = -kk * jnp.exp(b - logw)
    rt_ref[...] = r * jnp.exp(b)
    ph_ref[...] = pa * enb
    kh_ref[...] = k2 * enb
    pa_ref[...] = pa
    k2_ref[...] = k2
    v_ref[...] = v
    b_ref[...] = b

    mk = _pair_masks()
    lo, causal, strict, bd, eye = mk["lo"], mk["causal"], mk["strict"], mk["bd"], mk["eye"]
    tiles = _pair_tiles()

    def prep_body(i, carry):
        cs = [i * PREP_CHUNKS + j for j in range(PREP_CHUNKS)]
        rws = [pl.ds(pl.multiple_of(c * CHUNK, CHUNK), CHUNK) for c in cs]
        bl = [b_ref[pl.ds(pl.multiple_of(c * CHUNK, CHUNK) + CHUNK - 1, 1), :] for c in cs]
        ebl = [jnp.exp(b1 - b_ref[rows, :]) for b1, rows in zip(bl, rws)]
        pt_c = [pa_ref[rows, :] * e for rows, e in zip(rws, ebl)]
        kt_c = [k2_ref[rows, :] * e for rows, e in zip(rws, ebl)]
        wl = [jnp.exp(b1) for b1 in bl]
        items = [(j, p) for j in range(PREP_CHUNKS) for p in range(PAIRS)]
        qt = [qt_ref[rws[j], tiles[p]] for j, p in items]
        rt = [rt_ref[rws[j], tiles[p]] for j, p in items]
        v_t = [v_ref[rws[j], tiles[p]] for j, p in items]
        qr = [jnp.concatenate([a, b], axis=0) for a, b in zip(qt, rt)]
        aap = [_dot_nt(t, _stack(ph_ref[rws[j], tiles[p]], lo)) for (j, p), t in zip(items, qr)]
        aak = [_dot_nt(t, _stack(kh_ref[rws[j], tiles[p]], lo)) for (j, p), t in zip(items, qr)]
        av = [_dot(jnp.concatenate([jnp.where(strict, t[0:CHUNK], 0.0), jnp.where(causal, t[CHUNK:], 0.0)],
                                   axis=0), _stack(v, lo)) for t, v in zip(aak, v_t)]
        tinv = _bd_inverse_all([_stack(jnp.where(strict, t[0:CHUNK], 0.0), lo) for t in aap], eye)
        qu = [_dot(t, jnp.concatenate([_stack(a, lo), _stack(b[0:CHUNK], lo)], axis=1))
              for t, a, b in zip(tinv, qt, av)]
        t1 = [_dot(jnp.where(causal, t[CHUNK:], 0.0), y) for t, y in zip(aap, qu)]
        qu_t = [y[0:CHUNK] + y[CHUNK:] for y in qu]
        t2 = [_dot_tn(pt_c[j][:, tiles[p]], y[:, 0:LANES]) for (j, p), y in zip(items, qu_t)]
        t3 = [_dot_tn(jnp.concatenate([pt_c[j][:, tiles[p]], kt_c[j][:, tiles[p]]], axis=0),
                      jnp.concatenate([y[:, LANES:], v], axis=0))
              for (j, p), y, v in zip(items, qu_t, v_t)]
        for n, (j, p) in enumerate(items):
            qm_ref[cs[j], p, 0:CHUNK, :] = rt[n] + t1[n][:, 0:LANES]
            qm_ref[cs[j], p, CHUNK:, :] = eye * wl[j][:, tiles[p]] + jnp.where(bd, t2[n], 0.0)
            n_ref[cs[j], p] = jnp.where(bd, t3[n], 0.0)
            o_ref[rws[j], tiles[p]] = t1[n][:, LANES:] + av[n][CHUNK:]
        return carry

    lax.fori_loop(0, tb // (CHUNK * PREP_CHUNKS), prep_body, 0)

    def scan_body(c, carry):
        rows = pl.ds(pl.multiple_of(c * CHUNK, CHUNK), CHUNK)
        out = [_dot(qm_ref[c, p], s_ref[p]) for p in range(PAIRS)]
        for p in range(PAIRS):
            s_ref[p] = out[p][CHUNK:] + n_ref[c, p]
            o_ref[rows, tiles[p]] = o_ref[rows, tiles[p]] + out[p][0:CHUNK]
        return carry

    lax.fori_loop(0, tb // CHUNK, scan_body, 0)

    o = o_ref[...]
    inv_d = 1.0 / HEAD_DIM
    mean = _dot_sel(_dot_sel(o, red, SUM_PASSES) * inv_d, expd, EXPAND_PASSES)
    d = o - mean
    var = _dot_sel(d * d, red, SUM_PASSES) * inv_d
    on = d * _dot_sel(lax.rsqrt(var + RWKV_GN_EPS), expd, EXPAND_PASSES)
    on = on * lnw_ref[...] + lnb_ref[...]
    bonus = _dot_sel(_dot_sel(r * k2 * rk_ref[...], red, SUM_PASSES), expd, EXPAND_PASSES) * v
    y_ref[...] = (on + bonus) * _silu(gate)


def _hgrn_kernel(x_ref, nw_ref, w_ref, lbraw_ref, hnw_ref, cum_ref, lvl_ref, red_ref, expd_ref, y_ref,
                 s_ref, kin_ref, v_ref, b_ref, qe_ref, zl_ref, o_ref, *, layer, depth):
    tb = x_ref.shape[0]

    @pl.when(pl.program_id(0) == 0)
    def _():
        s_ref[...] = jnp.zeros_like(s_ref)

    lbs = [lbraw_ref[j:j + 1, :] for j in range(depth)]
    mx = lbs[0]
    for t in lbs[1:]:
        mx = jnp.maximum(mx, t)
    es = [jnp.exp(t - mx) for t in lbs]
    tot = es[0]
    for t in es[1:]:
        tot = tot + t
    soft = [t / tot for t in es]
    lb = soft[0]
    for t in soft[1:layer + 1]:
        lb = lb + t
    lb = lb - soft[0]

    p = _norm_proj(x_ref, nw_ref, w_ref)
    q = _silu(p[:, 0:WIDTH])
    f = p[:, WIDTH:2 * WIDTH]
    v = p[:, 2 * WIDTH:3 * WIDTH]
    gate = p[:, 3 * WIDTH:]
    log_g = jax.nn.log_sigmoid(f) + jnp.log1p(lb * jnp.exp(-f))
    kin = (1.0 - lb) * jax.nn.sigmoid(-f)

    b = _sel_dot(cum_ref[...], log_g, CUMSUM_PASSES)
    kin_ref[...] = kin
    v_ref[...] = v
    b_ref[...] = b
    qe_ref[...] = q * jnp.exp(b)
    tr = _iota((tb, WIDTH), 0)
    for l in range(LEVELS):
        upper = (tr & (1 << l)) != 0
        s = 1 << l
        if s == 1:
            d = jnp.where(upper, log_g, 0.0)
        elif 2 * s >= SUBLANES:
            b3 = b.reshape(tb // (2 * s), 2 * s, WIDTH)
            mid = jnp.broadcast_to(b3[:, s - 1:s, :], b3.shape).reshape(tb, WIDTH)
            d = jnp.where(upper, b - mid, mid - b)
        else:
            d = _sel_dot(lvl_ref[MATMUL_LEVELS.index(l)], log_g, CUMSUM_PASSES)
        zl_ref[l] = jnp.where(upper, q, kin) * jnp.exp(d)

    mk = _pair_masks()
    lo, bd = mk["lo"], mk["bd"]
    kj = _iota((LANES, CHUNK), 0) & (CHUNK - 1)
    qi = _iota((LANES, CHUNK), 1)
    lvl_masks_t = [((qi >> (l + 1)) == (kj >> (l + 1))) & ((qi & (1 << l)) != 0) & ((kj & (1 << l)) == 0)
                   for l in range(LEVELS)]
    tiles = _pair_tiles()

    def chunk_body(i, carry):
        cs = [i * HGRN_CHUNKS + j for j in range(HGRN_CHUNKS)]
        rws = [pl.ds(pl.multiple_of(c * CHUNK, CHUNK), CHUNK) for c in cs]
        bl = [b_ref[pl.ds(pl.multiple_of(c * CHUNK, CHUNK) + CHUNK - 1, 1), :] for c in cs]
        kt_c = [kin_ref[rows, :] * jnp.exp(b1 - b_ref[rows, :]) for b1, rows in zip(bl, rws)]
        wl = [jnp.exp(b1) for b1 in bl]
        st = [s_ref[p] for p in range(PAIRS)]
        for j in range(HGRN_CHUNKS):
            att = [None] * PAIRS
            for l in range(LEVELS):
                for p in range(PAIRS):
                    z = zl_ref[l, rws[j], tiles[p]]
                    t = jnp.where(lvl_masks_t[l], _dot_nt(_stack(z, lo), z), 0.0)
                    att[p] = t if att[p] is None else att[p] + t
            v_t = [v_ref[rws[j], tiles[p]] for p in range(PAIRS)]
            oi = [_dot_tn(a, _stack(v, lo)) for a, v in zip(att, v_t)]
            sn = [jnp.where(bd, _dot_tn(v, kt_c[j][:, tiles[p]]), 0.0) for p, v in enumerate(v_t)]
            for p in range(PAIRS):
                o_ref[rws[j], tiles[p]] = oi[p] + _dot_nt(qe_ref[rws[j], tiles[p]], st[p])
                st[p] = st[p] * wl[j][:, tiles[p]] + sn[p]
        for p in range(PAIRS):
            s_ref[p] = st[p]
        return carry

    lax.fori_loop(0, tb // (CHUNK * HGRN_CHUNKS), chunk_body, 0)

    o = o_ref[...] + _dot_sel(_dot_sel(q * kin, red_ref[...], SUM_PASSES), expd_ref[...], EXPAND_PASSES) * v
    ms = jnp.mean(o * o, axis=-1, keepdims=True)
    y_ref[...] = (o * lax.rsqrt(ms + NORM_EPS)) * hnw_ref[...] * _silu(gate)


def _outproj_kernel(x_ref, ya_ref, yb_ref, yc_ref, w_ref, fnw_ref, o_ref, *, final):
    acc = x_ref[...]
    for i, y_ref in enumerate((ya_ref, yb_ref, yc_ref)):
        acc = acc + _dot(y_ref[...].astype(BF16), w_ref[i * WIDTH:(i + 1) * WIDTH, :])
    if final:
        ms = jnp.mean(acc * acc, axis=-1, keepdims=True)
        acc = (acc * lax.rsqrt(ms + NORM_EPS)) * fnw_ref[...]
    o_ref[...] = acc


def _full(shape):
    nd = len(shape)
    return pl.BlockSpec(shape, lambda i, _nd=nd: (0,) * _nd)


def _rows(tb, width):
    return pl.BlockSpec((tb, width), lambda i: (i, 0))


def _mixer_call(body, name, x2, params, scratch):
    t, d = x2.shape
    tb = TIME_BLOCK
    return pl.pallas_call(
        body,
        grid=(t // tb,),
        in_specs=[_rows(tb, d)] + [_full(a.shape) for a in params],
        out_specs=_rows(tb, WIDTH),
        out_shape=jax.ShapeDtypeStruct((t, WIDTH), F32),
        scratch_shapes=scratch,
        compiler_params=pltpu.CompilerParams(dimension_semantics=("arbitrary",),
                                             vmem_limit_bytes=VMEM_LIMIT),
        name=name,
    )(x2, *params)


def _slab(tb):
    return pltpu.VMEM((tb, WIDTH), F32)


def _state():
    return pltpu.VMEM((PAIRS, LANES, LANES), F32)


def _folded(nc):
    return [pltpu.VMEM((nc, PAIRS, CHUNK + LANES, LANES), F32), pltpu.VMEM((nc, PAIRS, LANES, LANES), F32)]


def _row(a):
    return a.reshape(1, -1).astype(F32)


def _lane_pad(a, offset):
    return jnp.zeros((1, LANES), F32).at[0, offset:offset + HEADS].set(a.astype(F32))


def kernel(x, norm_w, w_in, gdn_conv_w, gdn_a_log, gdn_dt_bias, gdn_norm_w, rwkv_mu, rwkv_w0, rwkv_w_up, rwkv_a0, rwkv_a_up, rwkv_k_k, rwkv_k_a, rwkv_r_k, rwkv_ln_w, rwkv_ln_b, hgrn_lower_bounds, hgrn_norm_w, w_out, final_norm_w):
    bsz, seq, d = x.shape
    depth = norm_w.shape[0]
    assert bsz == 1 and seq % TIME_BLOCK == 0 and seq % OUT_BLOCK == 0
    tb = TIME_BLOCK
    nc = tb // CHUNK
    x2 = x.reshape(seq, d)

    gdn_cols = 3 * WIDTH + 2 * HEADS + WIDTH
    rwkv_off = gdn_cols
    hgrn_off = gdn_cols + RWKV_PROJ

    cum = _const(_np_chunk_cumsum(CONST_ROWS))
    red = _const(_np_head_reduce())
    exp0 = _const(_np_head_expand(0))
    exp8 = _const(_np_head_expand(HEADS))
    lvl = _const(np.stack([_np_level_matrix(CONST_ROWS, 1 << l) for l in MATMUL_LEVELS]))

    for l in range(depth):
        wl = w_in[l]
        nw = _row(norm_w[l])
        w_gdn = jnp.concatenate(
            [wl[:, 0:3 * WIDTH], wl[:, 3 * WIDTH + 2 * HEADS:gdn_cols],
             wl[:, 3 * WIDTH:3 * WIDTH + 2 * HEADS],
             jnp.zeros((d, LANES - 2 * HEADS), wl.dtype)], axis=1).astype(BF16)
        w_rwkv = wl[:, rwkv_off:rwkv_off + RWKV_PROJ].astype(BF16)
        w_hgrn = wl[:, hgrn_off:hgrn_off + HGRN_PROJ].astype(BF16)

        y_a = _mixer_call(
            _gdn_kernel, f"gdn_{l}", x2,
            [nw, w_gdn, gdn_conv_w[l].astype(F32), _lane_pad(gdn_a_log[l], HEADS),
             _lane_pad(gdn_dt_bias[l], HEADS), _row(jnp.tile(gdn_norm_w[l], HEADS)),
             cum, red, exp0, exp8],
            [_state(), pltpu.VMEM((tb + SUBLANES, 3 * WIDTH), F32)] + [_slab(tb)] * 7
            + [pltpu.VMEM((tb, LANES), F32)] + _folded(nc) + [_slab(tb)])

        y_b = _mixer_call(
            _rwkv_kernel, f"rwkv_{l}", x2,
            [nw, w_rwkv, _row(rwkv_mu[l]), _row(rwkv_w0[l]), rwkv_w_up[l].astype(F32),
             _row(rwkv_a0[l]), rwkv_a_up[l].astype(F32), _row(rwkv_k_k[l]), _row(rwkv_k_a[l]),
             _row(rwkv_r_k[l]), _row(rwkv_ln_w[l]), _row(rwkv_ln_b[l]), cum, red, exp0],
            [_state(), pltpu.VMEM((tb + SUBLANES, RWKV_PROJ), F32)] + [_slab(tb)] * 8
            + _folded(nc) + [_slab(tb)])

        y_c = _mixer_call(
            functools.partial(_hgrn_kernel, layer=l, depth=depth), f"hgrn_{l}", x2,
            [nw, w_hgrn, hgrn_lower_bounds.astype(F32), _row(hgrn_norm_w[l]), cum, lvl, red, exp0],
            [_state()] + [_slab(tb)] * 4 + [pltpu.VMEM((LEVELS, tb, WIDTH), F32), _slab(tb)])

        final = l == depth - 1
        x2 = pl.pallas_call(
            functools.partial(_outproj_kernel, final=final),
            grid=(seq // OUT_BLOCK,),
            in_specs=[_rows(OUT_BLOCK, d)] + [_rows(OUT_BLOCK, WIDTH)] * 3
            + [_full((3 * WIDTH, d)), _full((1, d))],
            out_specs=_rows(OUT_BLOCK, d),
            out_shape=jax.ShapeDtypeStruct((seq, d), F32),
            compiler_params=pltpu.CompilerParams(dimension_semantics=("arbitrary",),
                                                 vmem_limit_bytes=VMEM_LIMIT),
            name=f"outproj_{l}",
        )(x2, y_a, y_b, y_c, w_out[l].astype(BF16), _row(final_norm_w))

    return x2.reshape(bsz, seq, d)
```

```python
import functools

import numpy as np
import jax
import jax.numpy as jnp
from jax import lax
from jax.experimental import pallas as pl
from jax.experimental.pallas import tpu as pltpu

F32 = jnp.float32
BF16 = jnp.bfloat16

HEAD_DIM = 64
HEADS = 8
PAIRS = HEADS // 2
WIDTH = HEADS * HEAD_DIM
CHUNK = 64
LEVELS = CHUNK.bit_length() - 1
CONV_WIDTH = 4
LORA = 64
NORM_EPS = 1e-6
L2_EPS = 1e-6
RWKV_GN_EPS = 64e-5
LANES = 128
SUBLANES = 8
MATMUL_LEVELS = [l for l in range(1, LEVELS) if 2 * (1 << l) < SUBLANES]
RWKV_PROJ = 4 * WIDTH + 2 * LORA
HGRN_PROJ = 4 * WIDTH
PROJ_BLOCK = 4 * WIDTH + LANES
TIME_BLOCK = 512
CONST_ROWS = 256
PREP_CHUNKS = 4
HGRN_CHUNKS = 4
SUM_PASSES = 1
EXPAND_PASSES = 1
CUMSUM_PASSES = 2
GDN_DECAY_PASSES = 3
OUT_BLOCK = 512
VMEM_LIMIT = 56 * 1024 * 1024


def _dot(a, b):
    return jnp.dot(a, b, preferred_element_type=F32)


def _dot_nt(a, b):
    return lax.dot_general(a, b, (((1,), (1,)), ((), ())), preferred_element_type=F32)


def _dot_tn(a, b):
    return lax.dot_general(a, b, (((0,), (0,)), ((), ())), preferred_element_type=F32)


def _split_bf16(x, n):
    parts, r = [], x
    for i in range(n):
        p = r.astype(BF16)
        parts.append(p)
        if i + 1 < n:
            r = r - p.astype(F32)
    return parts


def _sel_dot(m01, x, n):
    parts = _split_bf16(x, n)
    blocks = []
    for r0 in range(0, x.shape[0], CONST_ROWS):
        out = None
        for p in parts:
            t = _dot(m01, p[r0:r0 + CONST_ROWS])
            out = t if out is None else out + t
        blocks.append(out)
    return blocks[0] if len(blocks) == 1 else jnp.concatenate(blocks, axis=0)


def _dot_sel(x, m01, n):
    out = None
    for p in _split_bf16(x, n):
        t = _dot(p, m01)
        out = t if out is None else out + t
    return out


def _iota(shape, dim):
    return lax.broadcasted_iota(jnp.int32, shape, dim)


def _bd_inverse_all(ps, eye_bd):
    xs = [eye_bd + p for p in ps]
    ps = [_dot(p, p) for p in ps]
    for _ in range(LEVELS - 2):
        r = [_dot(p, jnp.concatenate([x, p], axis=1)) for p, x in zip(ps, xs)]
        xs = [x + t[:, 0:LANES] for x, t in zip(xs, r)]
        ps = [t[:, LANES:] for t in r]
    return [x + _dot(p, x) for p, x in zip(ps, xs)]


def _pair_masks():
    i = _iota((CHUNK, LANES), 0)
    c = _iota((CHUNK, LANES), 1)
    cm = c & (HEAD_DIM - 1)
    r = _iota((LANES, LANES), 0)
    l = _iota((LANES, LANES), 1)
    return dict(lo=c < HEAD_DIM, causal=cm <= i, strict=cm < i,
                bd=(r // HEAD_DIM) == (l // HEAD_DIM), eye=jnp.where(r == l, 1.0, 0.0))


def _stack(x, lo):
    return jnp.concatenate([jnp.where(lo, x, 0.0), jnp.where(lo, 0.0, x)], axis=0)


def _pair_tiles():
    return [slice(p * LANES, (p + 1) * LANES) for p in range(PAIRS)]


def _silu(x):
    return x * jax.nn.sigmoid(x)


def _norm_proj(x_ref, nw_ref, w_ref, cols=PROJ_BLOCK):
    x = x_ref[...]
    ms = jnp.mean(x * x, axis=-1, keepdims=True)
    h = (x * lax.rsqrt(ms + NORM_EPS)) * nw_ref[...]
    return _dot(h.astype(BF16), w_ref[:, 0:cols])


def _np_head_expand(offset):
    r = np.arange(LANES)[:, None]
    c = np.arange(WIDTH)[None, :]
    return r == c // HEAD_DIM + offset


def _np_head_reduce():
    return _np_head_expand(0).T


def _np_chunk_cumsum(n):
    r = np.arange(n)[:, None]
    c = np.arange(n)[None, :]
    return (r // CHUNK == c // CHUNK) & (c <= r)


def _np_level_matrix(n, s):
    r = np.arange(n)[:, None]
    c = np.arange(n)[None, :]
    tr, tc = r % CHUNK, c % CHUNK
    mid = (tr // (2 * s)) * (2 * s) + s - 1
    upper = (tr // s) % 2 == 1
    m = (upper & (tc > mid) & (tc <= tr)) | (~upper & (tc > tr) & (tc <= mid))
    return (r // CHUNK == c // CHUNK) & m


def _const(mask):
    return jnp.asarray(np.asarray(mask, np.float32), BF16)


def _gdn_kernel(x_ref, w_ref, nw_ref, conv_ref, alog_ref, dtb_ref, gnw_ref, cum_ref, red_ref, expb_ref,
                expg_ref, y_ref,
                s_ref, ext_ref, q_ref, k_ref, kb_ref, vb_ref, kbe_ref, qd_ref, gcf_ref, gcs_ref,
                qm_ref, n_ref, o_ref):
    tb = x_ref.shape[0]

    @pl.when(pl.program_id(0) == 0)
    def _():
        s_ref[...] = jnp.zeros_like(s_ref)
        ext_ref[0:SUBLANES, :] = jnp.zeros((SUBLANES, 3 * WIDTH), F32)

    p = _norm_proj(x_ref, nw_ref, w_ref)
    gate = p[:, 3 * WIDTH:4 * WIDTH]
    small = p[:, 4 * WIDTH:4 * WIDTH + LANES]

    ext_ref[SUBLANES:SUBLANES + tb, :] = p[:, 0:3 * WIDTH]
    conv = None
    for j in range(CONV_WIDTH):
        t = conv_ref[j:j + 1, :] * ext_ref[pl.ds(SUBLANES - (CONV_WIDTH - 1) + j, tb), :]
        conv = t if conv is None else conv + t
    ext_ref[0:SUBLANES, :] = ext_ref[tb:tb + SUBLANES, :]
    qkv = _silu(conv)
    q, k, v = qkv[:, 0:WIDTH], qkv[:, WIDTH:2 * WIDTH], qkv[:, 2 * WIDTH:3 * WIDTH]

    red = red_ref[...]
    exp_b = expb_ref[...]
    q_rs = lax.rsqrt(_dot_sel(q * q, red, SUM_PASSES) + L2_EPS)
    k_rs = lax.rsqrt(_dot_sel(k * k, red, SUM_PASSES) + L2_EPS)
    qn = (q * _dot_sel(q_rs, exp_b, EXPAND_PASSES)) * (HEAD_DIM ** -0.5)
    kn = k * _dot_sel(k_rs, exp_b, EXPAND_PASSES)

    beta = _dot_sel(jax.nn.sigmoid(small), exp_b, EXPAND_PASSES)
    g_small = -jnp.exp(alog_ref[...]) * jax.nn.softplus(small + dtb_ref[...])
    gc_small = _sel_dot(cum_ref[...], g_small, GDN_DECAY_PASSES)
    gc_full = _dot_sel(gc_small, expg_ref[...], GDN_DECAY_PASSES)
    egc = jnp.exp(gc_full)
    kb = kn * beta

    q_ref[...] = qn
    k_ref[...] = kn
    kb_ref[...] = kb
    vb_ref[...] = v * beta
    kbe_ref[...] = kb * egc
    qd_ref[...] = qn * egc
    gcf_ref[...] = gc_full
    gcs_ref[...] = gc_small

    mk = _pair_masks()
    lo, causal, strict, bd, eye = mk["lo"], mk["causal"], mk["strict"], mk["bd"], mk["eye"]
    tiles = _pair_tiles()

    def prep_body(i, carry):
        cs = [i * PREP_CHUNKS + j for j in range(PREP_CHUNKS)]
        rws = [pl.ds(pl.multiple_of(c * CHUNK, CHUNK), CHUNK) for c in cs]
        gc_c = [gcf_ref[rows, :] for rows in rws]
        gl = [gcf_ref[pl.ds(pl.multiple_of(c * CHUNK, CHUNK) + CHUNK - 1, 1), :] for c in cs]
        kd_c = [k_ref[rows, :] * jnp.exp(g1 - g2) for rows, g1, g2 in zip(rws, gl, gc_c)]
        last = [jnp.exp(g1) for g1 in gl]
        g_rows = [gcs_ref[rows, :].T for rows in rws]
        items = [(j, p) for j in range(PREP_CHUNKS) for p in range(PAIRS)]
        ak = [_dot_nt(jnp.concatenate([kb_ref[rws[j], tiles[p]], q_ref[rws[j], tiles[p]]], axis=0),
                      _stack(k_ref[rws[j], tiles[p]], lo)) for j, p in items]
        g_row = [jnp.concatenate([g_rows[j][HEADS + 2 * p:HEADS + 2 * p + 1, :],
                                  g_rows[j][HEADS + 2 * p + 1:HEADS + 2 * p + 2, :]], axis=1)
                 for j, p in items]
        dm = [jnp.where(causal, jnp.exp(jnp.where(causal, gc_c[j][:, tiles[p]] - g, 0.0)), 0.0)
              for (j, p), g in zip(items, g_row)]
        qk = [t[CHUNK:] * d for t, d in zip(ak, dm)]
        tm = _bd_inverse_all([_stack(jnp.where(strict, -(t[0:CHUNK] * d), 0.0), lo)
                              for t, d in zip(ak, dm)], eye)
        uw = [_dot(t, jnp.concatenate([_stack(vb_ref[rws[j], tiles[p]], lo),
                                       _stack(kbe_ref[rws[j], tiles[p]], lo)], axis=1))
              for (j, p), t in zip(items, tm)]
        t1 = [_dot(a, b) for a, b in zip(qk, uw)]
        t2 = [_dot_tn(kd_c[j][:, tiles[p]], b[0:CHUNK] + b[CHUNK:])
              for (j, p), b in zip(items, uw)]
        for n, (j, p) in enumerate(items):
            qm_ref[cs[j], p, 0:CHUNK, :] = qd_ref[rws[j], tiles[p]] - t1[n][:, LANES:]
            qm_ref[cs[j], p, CHUNK:, :] = eye * last[j][:, tiles[p]] - jnp.where(bd, t2[n][:, LANES:], 0.0)
            n_ref[cs[j], p] = jnp.where(bd, t2[n][:, 0:LANES], 0.0)
            o_ref[rws[j], tiles[p]] = t1[n][:, 0:LANES]
        return carry

    lax.fori_loop(0, tb // (CHUNK * PREP_CHUNKS), prep_body, 0)

    def scan_body(c, carry):
        rows = pl.ds(pl.multiple_of(c * CHUNK, CHUNK), CHUNK)
        out = [_dot(qm_ref[c, p], s_ref[p]) for p in range(PAIRS)]
        for p in range(PAIRS):
            s_ref[p] = out[p][CHUNK:] + n_ref[c, p]
            o_ref[rows, tiles[p]] = o_ref[rows, tiles[p]] + out[p][0:CHUNK]
        return carry

    lax.fori_loop(0, tb // CHUNK, scan_body, 0)

    o = o_ref[...]
    rs = lax.rsqrt(_dot_sel(o * o, red_ref[...], SUM_PASSES) * (1.0 / HEAD_DIM) + NORM_EPS)
    y_ref[...] = (o * _dot_sel(rs, expb_ref[...], EXPAND_PASSES)) * gnw_ref[...] * _silu(gate)


def _rwkv_kernel(x_ref, w_ref, nw_ref, mu_ref, w0_ref, wup_ref, a0_ref, aup_ref, kk_ref, ka_ref,
                 rk_ref, lnw_ref, lnb_ref, cum_ref, red_ref, expd_ref, y_ref,
                 s_ref, ext_ref, qt_ref, rt_ref, ph_ref, kh_ref, pa_ref, k2_ref, v_ref, b_ref,
                 qm_ref, n_ref, o_ref):
    tb = x_ref.shape[0]

    @pl.when(pl.program_id(0) == 0)
    def _():
        s_ref[...] = jnp.zeros_like(s_ref)
        ext_ref[0:SUBLANES, :] = jnp.zeros((SUBLANES, RWKV_PROJ), F32)

    p = _norm_proj(x_ref, nw_ref, w_ref)
    ext_ref[SUBLANES:SUBLANES + tb, :] = p
    prev = ext_ref[pl.ds(SUBLANES - 1, tb), :]
    ext_ref[0:SUBLANES, :] = ext_ref[tb:tb + SUBLANES, :]
    p = p + mu_ref[...] * (prev - p)
    r, k, v = p[:, 0:WIDTH], p[:, WIDTH:2 * WIDTH], p[:, 2 * WIDTH:3 * WIDTH]
    wd = p[:, 3 * WIDTH:3 * WIDTH + LORA]
    ad = p[:, 3 * WIDTH + LORA:3 * WIDTH + 2 * LORA]
    gate = p[:, 3 * WIDTH + 2 * LORA:]

    w_raw = -jax.nn.softplus(-(w0_ref[...] + _dot(jnp.tanh(wd), wup_ref[...]))) - 0.5
    logw = -jnp.exp(w_raw)
    a = jax.nn.sigmoid(a0_ref[...] + _dot(ad, aup_ref[...]))

    red = red_ref[...]
    expd = expd_ref[...]
    kkp = k * kk_ref[...]
    kk = kkp * _dot_sel(lax.rsqrt(_dot_sel(kkp * kkp, red, SUM_PASSES) + L2_EPS), expd, EXPAND_PASSES)
    k2 = k * (1.0 + (a - 1.0) * ka_ref[...])
    pa = kk * a

    b = _sel_dot(cum_ref[...], logw, CUMSUM_PASSES)
    enb = jnp.exp(-b)
    qt_ref[...] = -kk * jnp.exp(b - logw)
    rt_ref[...] = r * jnp.exp(b)
    ph_ref[...] = pa * enb
    kh_ref[...] = k2 * enb
    pa_ref[...] = pa
    k2_ref[...] = k2
    v_ref[...] = v
    b_ref[...] = b

    mk = _pair_masks()
    lo, causal, strict, bd, eye = mk["lo"], mk["causal"], mk["strict"], mk["bd"], mk["eye"]
    tiles = _pair_tiles()

    def prep_body(i, carry):
        cs = [i * PREP_CHUNKS + j for j in range(PREP_CHUNKS)]
        rws = [pl.ds(pl.multiple_of(c * CHUNK, CHUNK), CHUNK) for c in cs]
        bl = [b_ref[pl.ds(pl.multiple_of(c * CHUNK, CHUNK) + CHUNK - 1, 1), :] for c in cs]
        ebl = [jnp.exp(b1 - b_ref[rows, :]) for b1, rows in zip(bl, rws)]
        pt_c = [pa_ref[rows, :] * e for rows, e in zip(rws, ebl)]
        kt_c = [k2_ref[rows, :] * e for rows, e in zip(rws, ebl)]
        wl = [jnp.exp(b1) for b1 in bl]
        items = [(j, p) for j in range(PREP_CHUNKS) for p in range(PAIRS)]
        qt = [qt_ref[rws[j], tiles[p]] for j, p in items]
        rt = [rt_ref[rws[j], tiles[p]] for j, p in items]
        v_t = [v_ref[rws[j], tiles[p]] for j, p in items]
        qr = [jnp.concatenate([a, b], axis=0) for a, b in zip(qt, rt)]
        aap = [_dot_nt(t, _stack(ph_ref[rws[j], tiles[p]], lo)) for (j, p), t in zip(items, qr)]
        aak = [_dot_nt(t, _stack(kh_ref[rws[j], tiles[p]], lo)) for (j, p), t in zip(items, qr)]
        av = [_dot(jnp.concatenate([jnp.where(strict, t[0:CHUNK], 0.0), jnp.where(causal, t[CHUNK:], 0.0)],
                                   axis=0), _stack(v, lo)) for t, v in zip(aak, v_t)]
        tinv = _bd_inverse_all([_stack(jnp.where(strict, t[0:CHUNK], 0.0), lo) for t in aap], eye)
        qu = [_dot(t, jnp.concatenate([_stack(a, lo), _stack(b[0:CHUNK], lo)], axis=1))
              for t, a, b in zip(tinv, qt, av)]
        t1 = [_dot(jnp.where(causal, t[CHUNK:], 0.0), y) for t, y in zip(aap, qu)]
        qu_t = [y[0:CHUNK] + y[CHUNK:] for y in qu]
        t2 = [_dot_tn(pt_c[j][:, tiles[p]], y[:, 0:LANES]) for (j, p), y in zip(items, qu_t)]
        t3 = [_dot_tn(jnp.concatenate([pt_c[j][:, tiles[p]], kt_c[j][:, tiles[p]]], axis=0),
                      jnp.concatenate([y[:, LANES:], v], axis=0))
              for (j, p), y, v in zip(items, qu_t, v_t)]
        for n, (j, p) in enumerate(items):
            qm_ref[cs[j], p, 0:CHUNK, :] = rt[n] + t1[n][:, 0:LANES]
            qm_ref[cs[j], p, CHUNK:, :] = eye * wl[j][:, tiles[p]] + jnp.where(bd, t2[n], 0.0)
            n_ref[cs[j], p] = jnp.where(bd, t3[n], 0.0)
            o_ref[rws[j], tiles[p]] = t1[n][:, LANES:] + av[n][CHUNK:]
        return carry

    lax.fori_loop(0, tb // (CHUNK * PREP_CHUNKS), prep_body, 0)

    def scan_body(c, carry):
        rows = pl.ds(pl.multiple_of(c * CHUNK, CHUNK), CHUNK)
        out = [_dot(qm_ref[c, p], s_ref[p]) for p in range(PAIRS)]
        for p in range(PAIRS):
            s_ref[p] = out[p][CHUNK:] + n_ref[c, p]
            o_ref[rows, tiles[p]] = o_ref[rows, tiles[p]] + out[p][0:CHUNK]
        return carry

    lax.fori_loop(0, tb // CHUNK, scan_body, 0)

    o = o_ref[...]
    inv_d = 1.0 / HEAD_DIM
    mean = _dot_sel(_dot_sel(o, red, SUM_PASSES) * inv_d, expd, EXPAND_PASSES)
    d = o - mean
    var = _dot_sel(d * d, red, SUM_PASSES) * inv_d
    on = d * _dot_sel(lax.rsqrt(var + RWKV_GN_EPS), expd, EXPAND_PASSES)
    on = on * lnw_ref[...] + lnb_ref[...]
    bonus = _dot_sel(_dot_sel(r * k2 * rk_ref[...], red, SUM_PASSES), expd, EXPAND_PASSES) * v
    y_ref[...] = (on + bonus) * _silu(gate)


def _hgrn_kernel(x_ref, w_ref, nw_ref, lbraw_ref, hnw_ref, cum_ref, lvl_ref, red_ref, expd_ref, y_ref,
                 s_ref, kin_ref, v_ref, b_ref, qe_ref, zl_ref, o_ref, *, layer, depth):
    tb = x_ref.shape[0]

    @pl.when(pl.program_id(0) == 0)
    def _():
        s_ref[...] = jnp.zeros_like(s_ref)

    lbs = [lbraw_ref[j:j + 1, :] for j in range(depth)]
    mx = lbs[0]
    for t in lbs[1:]:
        mx = jnp.maximum(mx, t)
    es = [jnp.exp(t - mx) for t in lbs]
    tot = es[0]
    for t in es[1:]:
        tot = tot + t
    soft = [t / tot for t in es]
    lb = soft[0]
    for t in soft[1:layer + 1]:
        lb = lb + t
    lb = lb - soft[0]

    p = _norm_proj(x_ref, nw_ref, w_ref, HGRN_PROJ)
    q = _silu(p[:, 0:WIDTH])
    f = p[:, WIDTH:2 * WIDTH]
    v = p[:, 2 * WIDTH:3 * WIDTH]
    gate = p[:, 3 * WIDTH:]
    log_g = jax.nn.log_sigmoid(f) + jnp.log1p(lb * jnp.exp(-f))
    kin = (1.0 - lb) * jax.nn.sigmoid(-f)

    b = _sel_dot(cum_ref[...], log_g, CUMSUM_PASSES)
    kin_ref[...] = kin
    v_ref[...] = v
    b_ref[...] = b
    qe_ref[...] = q * jnp.exp(b)
    tr = _iota((tb, WIDTH), 0)
    for l in range(LEVELS):
        upper = (tr & (1 << l)) != 0
        s = 1 << l
        if s == 1:
            d = jnp.where(upper, log_g, 0.0)
        elif 2 * s >= SUBLANES:
            b3 = b.reshape(tb // (2 * s), 2 * s, WIDTH)
            mid = jnp.broadcast_to(b3[:, s - 1:s, :], b3.shape).reshape(tb, WIDTH)
            d = jnp.where(upper, b - mid, mid - b)
        else:
            d = _sel_dot(lvl_ref[MATMUL_LEVELS.index(l)], log_g, CUMSUM_PASSES)
        zl_ref[l] = jnp.where(upper, q, kin) * jnp.exp(d)

    mk = _pair_masks()
    lo, bd = mk["lo"], mk["bd"]
    kj = _iota((LANES, CHUNK), 0) & (CHUNK - 1)
    qi = _iota((LANES, CHUNK), 1)
    lvl_masks_t = [((qi >> (l + 1)) == (kj >> (l + 1))) & ((qi & (1 << l)) != 0) & ((kj & (1 << l)) == 0)
                   for l in range(LEVELS)]
    tiles = _pair_tiles()

    def chunk_body(i, carry):
        cs = [i * HGRN_CHUNKS + j for j in range(HGRN_CHUNKS)]
        rws = [pl.ds(pl.multiple_of(c * CHUNK, CHUNK), CHUNK) for c in cs]
        bl = [b_ref[pl.ds(pl.multiple_of(c * CHUNK, CHUNK) + CHUNK - 1, 1), :] for c in cs]
        kt_c = [kin_ref[rows, :] * jnp.exp(b1 - b_ref[rows, :]) for b1, rows in zip(bl, rws)]
        wl = [jnp.exp(b1) for b1 in bl]
        st = [s_ref[p] for p in range(PAIRS)]
        for j in range(HGRN_CHUNKS):
            att = [None] * PAIRS
            for l in range(LEVELS):
                for p in range(PAIRS):
                    z = zl_ref[l, rws[j], tiles[p]]
                    t = jnp.where(lvl_masks_t[l], _dot_nt(_stack(z, lo), z), 0.0)
                    att[p] = t if att[p] is None else att[p] + t
            v_t = [v_ref[rws[j], tiles[p]] for p in range(PAIRS)]
            oi = [_dot_tn(a, _stack(v, lo)) for a, v in zip(att, v_t)]
            sn = [jnp.where(bd, _dot_tn(v, kt_c[j][:, tiles[p]]), 0.0) for p, v in enumerate(v_t)]
            for p in range(PAIRS):
                o_ref[rws[j], tiles[p]] = oi[p] + _dot_nt(qe_ref[rws[j], tiles[p]], st[p])
                st[p] = st[p] * wl[j][:, tiles[p]] + sn[p]
        for p in range(PAIRS):
            s_ref[p] = st[p]
        return carry

    lax.fori_loop(0, tb // (CHUNK * HGRN_CHUNKS), chunk_body, 0)

    o = o_ref[...] + _dot_sel(_dot_sel(q * kin, red_ref[...], SUM_PASSES), expd_ref[...], EXPAND_PASSES) * v
    ms = jnp.mean(o * o, axis=-1, keepdims=True)
    y_ref[...] = (o * lax.rsqrt(ms + NORM_EPS)) * hnw_ref[...] * _silu(gate)


def _outproj_kernel(x_ref, ya_ref, yb_ref, yc_ref, w_ref, fnw_ref, o_ref, *, final):
    acc = x_ref[...]
    for i, y_ref in enumerate((ya_ref, yb_ref, yc_ref)):
        acc = acc + _dot(y_ref[...].astype(BF16), w_ref[i * WIDTH:(i + 1) * WIDTH, :])
    if final:
        ms = jnp.mean(acc * acc, axis=-1, keepdims=True)
        acc = (acc * lax.rsqrt(ms + NORM_EPS)) * fnw_ref[...]
    o_ref[...] = acc


def _full(shape):
    nd = len(shape)
    return pl.BlockSpec(shape, lambda i, _nd=nd: (0,) * _nd)


def _rows(tb, width):
    return pl.BlockSpec((tb, width), lambda i: (i, 0))


def _mixer_call(body, name, x2, w_all, layer, block, params, scratch):
    t, d = x2.shape
    tb = TIME_BLOCK
    w_spec = pl.BlockSpec((None, d, PROJ_BLOCK), lambda i: (layer, 0, block))
    return pl.pallas_call(
        body,
        grid=(t // tb,),
        in_specs=[_rows(tb, d), w_spec] + [_full(a.shape) for a in params],
        out_specs=_rows(tb, WIDTH),
        out_shape=jax.ShapeDtypeStruct((t, WIDTH), F32),
        scratch_shapes=scratch,
        compiler_params=pltpu.CompilerParams(dimension_semantics=("arbitrary",),
                                             vmem_limit_bytes=VMEM_LIMIT),
        name=name,
    )(x2, w_all, *params)


def _slab(tb):
    return pltpu.VMEM((tb, WIDTH), F32)


def _state():
    return pltpu.VMEM((PAIRS, LANES, LANES), F32)


def _folded(nc):
    return [pltpu.VMEM((nc, PAIRS, CHUNK + LANES, LANES), F32), pltpu.VMEM((nc, PAIRS, LANES, LANES), F32)]


def _row(a):
    return a.reshape(1, -1).astype(F32)


def _lane_pad(a, offset):
    return jnp.zeros((1, LANES), F32).at[0, offset:offset + HEADS].set(a.astype(F32))


def kernel(x, norm_w, w_in, gdn_conv_w, gdn_a_log, gdn_dt_bias, gdn_norm_w, rwkv_mu, rwkv_w0, rwkv_w_up, rwkv_a0, rwkv_a_up, rwkv_k_k, rwkv_k_a, rwkv_r_k, rwkv_ln_w, rwkv_ln_b, hgrn_lower_bounds, hgrn_norm_w, w_out, final_norm_w):
    bsz, seq, d = x.shape
    depth = norm_w.shape[0]
    assert bsz == 1 and seq % TIME_BLOCK == 0 and seq % OUT_BLOCK == 0
    tb = TIME_BLOCK
    nc = tb // CHUNK
    x2 = x.reshape(seq, d)

    gdn_cols = 3 * WIDTH + 2 * HEADS + WIDTH
    rwkv_off = gdn_cols
    hgrn_off = gdn_cols + RWKV_PROJ

    cum = _const(_np_chunk_cumsum(CONST_ROWS))
    red = _const(_np_head_reduce())
    exp0 = _const(_np_head_expand(0))
    exp8 = _const(_np_head_expand(HEADS))
    lvl = _const(np.stack([_np_level_matrix(CONST_ROWS, 1 << l) for l in MATMUL_LEVELS]))

    wb = w_in.astype(BF16)

    def zeros(n):
        return jnp.zeros((depth, d, n), BF16)

    w_all = jnp.concatenate(
        [wb[:, :, 0:3 * WIDTH], wb[:, :, 3 * WIDTH + 2 * HEADS:gdn_cols],
         wb[:, :, 3 * WIDTH:3 * WIDTH + 2 * HEADS], zeros(LANES - 2 * HEADS),
         wb[:, :, rwkv_off:rwkv_off + RWKV_PROJ],
         wb[:, :, hgrn_off:hgrn_off + HGRN_PROJ], zeros(PROJ_BLOCK - HGRN_PROJ)], axis=2)
    w_out_b = w_out.astype(BF16)

    for l in range(depth):
        nw = _row(norm_w[l])

        y_a = _mixer_call(
            _gdn_kernel, f"gdn_{l}", x2, w_all, l, 0,
            [nw, gdn_conv_w[l].astype(F32), _lane_pad(gdn_a_log[l], HEADS),
             _lane_pad(gdn_dt_bias[l], HEADS), _row(jnp.tile(gdn_norm_w[l], HEADS)),
             cum, red, exp0, exp8],
            [_state(), pltpu.VMEM((tb + SUBLANES, 3 * WIDTH), F32)] + [_slab(tb)] * 7
            + [pltpu.VMEM((tb, LANES), F32)] + _folded(nc) + [_slab(tb)])

        y_b = _mixer_call(
            _rwkv_kernel, f"rwkv_{l}", x2, w_all, l, 1,
            [nw, _row(rwkv_mu[l]), _row(rwkv_w0[l]), rwkv_w_up[l].astype(F32),
             _row(rwkv_a0[l]), rwkv_a_up[l].astype(F32), _row(rwkv_k_k[l]), _row(rwkv_k_a[l]),
             _row(rwkv_r_k[l]), _row(rwkv_ln_w[l]), _row(rwkv_ln_b[l]), cum, red, exp0],
            [_state(), pltpu.VMEM((tb + SUBLANES, RWKV_PROJ), F32)] + [_slab(tb)] * 8
            + _folded(nc) + [_slab(tb)])

        y_c = _mixer_call(
            functools.partial(_hgrn_kernel, layer=l, depth=depth), f"hgrn_{l}", x2, w_all, l, 2,
            [nw, hgrn_lower_bounds.astype(F32), _row(hgrn_norm_w[l]), cum, lvl, red, exp0],
            [_state()] + [_slab(tb)] * 4 + [pltpu.VMEM((LEVELS, tb, WIDTH), F32), _slab(tb)])

        final = l == depth - 1
        x2 = pl.pallas_call(
            functools.partial(_outproj_kernel, final=final),
            grid=(seq // OUT_BLOCK,),
            in_specs=[_rows(OUT_BLOCK, d)] + [_rows(OUT_BLOCK, WIDTH)] * 3
            + [_full((3 * WIDTH, d)), _full((1, d))],
            out_specs=_rows(OUT_BLOCK, d),
            out_shape=jax.ShapeDtypeStruct((seq, d), F32),
            compiler_params=pltpu.CompilerParams(dimension_semantics=("arbitrary",),
                                                 vmem_limit_bytes=VMEM_LIMIT),
            name=f"outproj_{l}",
        )(x2, y_a, y_b, y_c, w_out_b[l], _row(final_norm_w))

    return x2.reshape(bsz, seq, d)
```

```python
import functools

import numpy as np
import jax
import jax.numpy as jnp
from jax import lax
from jax.experimental import pallas as pl
from jax.experimental.pallas import tpu as pltpu

F32 = jnp.float32
BF16 = jnp.bfloat16

HEAD_DIM = 64
HEADS = 8
PAIRS = HEADS // 2
WIDTH = HEADS * HEAD_DIM
CHUNK = 64
LEVELS = CHUNK.bit_length() - 1
CONV_WIDTH = 4
LORA = 64
NORM_EPS = 1e-6
L2_EPS = 1e-6
RWKV_GN_EPS = 64e-5
LANES = 128
SUBLANES = 8
MATMUL_LEVELS = [l for l in range(1, LEVELS) if 2 * (1 << l) < SUBLANES]
RWKV_PROJ = 4 * WIDTH + 2 * LORA
HGRN_PROJ = 4 * WIDTH
PROJ_BLOCK = 4 * WIDTH + LANES
TIME_BLOCK = 512
CONST_ROWS = 256
PREP_CHUNKS = 4
HGRN_CHUNKS = 4
SUM_PASSES = 1
EXPAND_PASSES = 1
CUMSUM_PASSES = 2
GDN_DECAY_PASSES = 3
OUT_BLOCK = 512
VMEM_LIMIT = 56 * 1024 * 1024


def _dot(a, b):
    return jnp.dot(a, b, preferred_element_type=F32)


def _dot_nt(a, b):
    return lax.dot_general(a, b, (((1,), (1,)), ((), ())), preferred_element_type=F32)


def _dot_tn(a, b):
    return lax.dot_general(a, b, (((0,), (0,)), ((), ())), preferred_element_type=F32)


def _split_bf16(x, n):
    parts, r = [], x
    for i in range(n):
        p = r.astype(BF16)
        parts.append(p)
        if i + 1 < n:
            r = r - p.astype(F32)
    return parts


def _sel_dot(m01, x, n):
    parts = _split_bf16(x, n)
    blocks = []
    for r0 in range(0, x.shape[0], CONST_ROWS):
        out = None
        for p in parts:
            t = _dot(m01, p[r0:r0 + CONST_ROWS])
            out = t if out is None else out + t
        blocks.append(out)
    return blocks[0] if len(blocks) == 1 else jnp.concatenate(blocks, axis=0)


def _dot_sel(x, m01, n):
    out = None
    for p in _split_bf16(x, n):
        t = _dot(p, m01)
        out = t if out is None else out + t
    return out


def _iota(shape, dim):
    return lax.broadcasted_iota(jnp.int32, shape, dim)


def _bd_inverse_all(ps, eye_bd):
    xs = [eye_bd + p for p in ps]
    ps = [_dot(p, p) for p in ps]
    for _ in range(LEVELS - 2):
        r = [_dot(p, jnp.concatenate([x, p], axis=1)) for p, x in zip(ps, xs)]
        xs = [x + t[:, 0:LANES] for x, t in zip(xs, r)]
        ps = [t[:, LANES:] for t in r]
    return [x + _dot(p, x) for p, x in zip(ps, xs)]


def _pair_masks():
    i = _iota((CHUNK, LANES), 0)
    c = _iota((CHUNK, LANES), 1)
    cm = c & (HEAD_DIM - 1)
    r = _iota((LANES, LANES), 0)
    l = _iota((LANES, LANES), 1)
    return dict(lo=c < HEAD_DIM, causal=cm <= i, strict=cm < i,
                bd=(r // HEAD_DIM) == (l // HEAD_DIM), eye=jnp.where(r == l, 1.0, 0.0))


def _stack(x, lo):
    return jnp.concatenate([jnp.where(lo, x, 0.0), jnp.where(lo, 0.0, x)], axis=0)


def _pair_tiles():
    return [slice(p * LANES, (p + 1) * LANES) for p in range(PAIRS)]


def _silu(x):
    return x * jax.nn.sigmoid(x)


def _norm_proj(x_ref, nw_ref, w_ref):
    x = x_ref[...]
    ms = jnp.mean(x * x, axis=-1, keepdims=True)
    h = (x * lax.rsqrt(ms + NORM_EPS)) * nw_ref[...]
    return _dot(h.astype(BF16), w_ref[...])


def _np_head_expand(offset):
    r = np.arange(LANES)[:, None]
    c = np.arange(WIDTH)[None, :]
    return r == c // HEAD_DIM + offset


def _np_head_reduce():
    return _np_head_expand(0).T


def _np_chunk_cumsum(n):
    r = np.arange(n)[:, None]
    c = np.arange(n)[None, :]
    return (r // CHUNK == c // CHUNK) & (c <= r)


def _np_level_matrix(n, s):
    r = np.arange(n)[:, None]
    c = np.arange(n)[None, :]
    tr, tc = r % CHUNK, c % CHUNK
    mid = (tr // (2 * s)) * (2 * s) + s - 1
    upper = (tr // s) % 2 == 1
    m = (upper & (tc > mid) & (tc <= tr)) | (~upper & (tc > tr) & (tc <= mid))
    return (r // CHUNK == c // CHUNK) & m


def _const(mask):
    return jnp.asarray(np.asarray(mask, np.float32), BF16)


def _gdn_kernel(x_ref, w_ref, nw_ref, conv_ref, alog_ref, dtb_ref, gnw_ref, cum_ref, red_ref, expb_ref,
                expg_ref, y_ref,
                s_ref, ext_ref, q_ref, k_ref, kb_ref, vb_ref, kbe_ref, qd_ref, gcf_ref, gcs_ref,
                qm_ref, n_ref, o_ref):
    tb = x_ref.shape[0]

    @pl.when(pl.program_id(0) == 0)
    def _():
        s_ref[...] = jnp.zeros_like(s_ref)
        ext_ref[0:SUBLANES, :] = jnp.zeros((SUBLANES, 3 * WIDTH), F32)

    p = _norm_proj(x_ref, nw_ref, w_ref)
    gate = p[:, 3 * WIDTH + 2 * HEADS:4 * WIDTH + 2 * HEADS]
    small = p[:, 3 * WIDTH:3 * WIDTH + LANES]

    ext_ref[SUBLANES:SUBLANES + tb, :] = p[:, 0:3 * WIDTH]
    conv = None
    for j in range(CONV_WIDTH):
        t = conv_ref[j:j + 1, :] * ext_ref[pl.ds(SUBLANES - (CONV_WIDTH - 1) + j, tb), :]
        conv = t if conv is None else conv + t
    ext_ref[0:SUBLANES, :] = ext_ref[tb:tb + SUBLANES, :]
    qkv = _silu(conv)
    q, k, v = qkv[:, 0:WIDTH], qkv[:, WIDTH:2 * WIDTH], qkv[:, 2 * WIDTH:3 * WIDTH]

    red = red_ref[...]
    exp_b = expb_ref[...]
    q_rs = lax.rsqrt(_dot_sel(q * q, red, SUM_PASSES) + L2_EPS)
    k_rs = lax.rsqrt(_dot_sel(k * k, red, SUM_PASSES) + L2_EPS)
    qn = (q * _dot_sel(q_rs, exp_b, EXPAND_PASSES)) * (HEAD_DIM ** -0.5)
    kn = k * _dot_sel(k_rs, exp_b, EXPAND_PASSES)

    beta = _dot_sel(jax.nn.sigmoid(small), exp_b, EXPAND_PASSES)
    g_small = -jnp.exp(alog_ref[...]) * jax.nn.softplus(small + dtb_ref[...])
    gc_small = _sel_dot(cum_ref[...], g_small, GDN_DECAY_PASSES)
    gc_full = _dot_sel(gc_small, expg_ref[...], GDN_DECAY_PASSES)
    egc = jnp.exp(gc_full)
    kb = kn * beta

    q_ref[...] = qn
    k_ref[...] = kn
    kb_ref[...] = kb
    vb_ref[...] = v * beta
    kbe_ref[...] = kb * egc
    qd_ref[...] = qn * egc
    gcf_ref[...] = gc_full
    gcs_ref[...] = gc_small

    mk = _pair_masks()
    lo, causal, strict, bd, eye = mk["lo"], mk["causal"], mk["strict"], mk["bd"], mk["eye"]
    tiles = _pair_tiles()

    def prep_body(i, carry):
        cs = [i * PREP_CHUNKS + j for j in range(PREP_CHUNKS)]
        rws = [pl.ds(pl.multiple_of(c * CHUNK, CHUNK), CHUNK) for c in cs]
        gc_c = [gcf_ref[rows, :] for rows in rws]
        gl = [gcf_ref[pl.ds(pl.multiple_of(c * CHUNK, CHUNK) + CHUNK - 1, 1), :] for c in cs]
        kd_c = [k_ref[rows, :] * jnp.exp(g1 - g2) for rows, g1, g2 in zip(rws, gl, gc_c)]
        last = [jnp.exp(g1) for g1 in gl]
        g_rows = [gcs_ref[rows, :].T for rows in rws]
        items = [(j, p) for j in range(PREP_CHUNKS) for p in range(PAIRS)]
        ak = [_dot_nt(jnp.concatenate([kb_ref[rws[j], tiles[p]], q_ref[rws[j], tiles[p]]], axis=0),
                      _stack(k_ref[rws[j], tiles[p]], lo)) for j, p in items]
        g_row = [jnp.concatenate([g_rows[j][HEADS + 2 * p:HEADS + 2 * p + 1, :],
                                  g_rows[j][HEADS + 2 * p + 1:HEADS + 2 * p + 2, :]], axis=1)
                 for j, p in items]
        dm = [jnp.where(causal, jnp.exp(jnp.where(causal, gc_c[j][:, tiles[p]] - g, 0.0)), 0.0)
              for (j, p), g in zip(items, g_row)]
        qk = [t[CHUNK:] * d for t, d in zip(ak, dm)]
        tm = _bd_inverse_all([_stack(jnp.where(strict, -(t[0:CHUNK] * d), 0.0), lo)
                              for t, d in zip(ak, dm)], eye)
        uw = [_dot(t, jnp.concatenate([_stack(vb_ref[rws[j], tiles[p]], lo),
                                       _stack(kbe_ref[rws[j], tiles[p]], lo)], axis=1))
              for (j, p), t in zip(items, tm)]
        t1 = [_dot(a, b) for a, b in zip(qk, uw)]
        t2 = [_dot_tn(kd_c[j][:, tiles[p]], b[0:CHUNK] + b[CHUNK:])
              for (j, p), b in zip(items, uw)]
        for n, (j, p) in enumerate(items):
            qm_ref[cs[j], p, 0:CHUNK, :] = qd_ref[rws[j], tiles[p]] - t1[n][:, LANES:]
            qm_ref[cs[j], p, CHUNK:, :] = eye * last[j][:, tiles[p]] - jnp.where(bd, t2[n][:, LANES:], 0.0)
            n_ref[cs[j], p] = jnp.where(bd, t2[n][:, 0:LANES], 0.0)
            o_ref[rws[j], tiles[p]] = t1[n][:, 0:LANES]
        return carry

    lax.fori_loop(0, tb // (CHUNK * PREP_CHUNKS), prep_body, 0)

    def scan_body(c, carry):
        rows = pl.ds(pl.multiple_of(c * CHUNK, CHUNK), CHUNK)
        out = [_dot(qm_ref[c, p], s_ref[p]) for p in range(PAIRS)]
        for p in range(PAIRS):
            s_ref[p] = out[p][CHUNK:] + n_ref[c, p]
            o_ref[rows, tiles[p]] = o_ref[rows, tiles[p]] + out[p][0:CHUNK]
        return carry

    lax.fori_loop(0, tb // CHUNK, scan_body, 0)

    o = o_ref[...]
    rs = lax.rsqrt(_dot_sel(o * o, red_ref[...], SUM_PASSES) * (1.0 / HEAD_DIM) + NORM_EPS)
    y_ref[...] = (o * _dot_sel(rs, expb_ref[...], EXPAND_PASSES)) * gnw_ref[...] * _silu(gate)


def _rwkv_kernel(x_ref, w_ref, nw_ref, mu_ref, w0_ref, wup_ref, a0_ref, aup_ref, kk_ref, ka_ref,
                 rk_ref, lnw_ref, lnb_ref, cum_ref, red_ref, expd_ref, y_ref,
                 s_ref, ext_ref, qt_ref, rt_ref, ph_ref, kh_ref, pa_ref, k2_ref, v_ref, b_ref,
                 qm_ref, n_ref, o_ref):
    tb = x_ref.shape[0]

    @pl.when(pl.program_id(0) == 0)
    def _():
        s_ref[...] = jnp.zeros_like(s_ref)
        ext_ref[0:SUBLANES, :] = jnp.zeros((SUBLANES, RWKV_PROJ), F32)

    p = _norm_proj(x_ref, nw_ref, w_ref)
    ext_ref[SUBLANES:SUBLANES + tb, :] = p
    prev = ext_ref[pl.ds(SUBLANES - 1, tb), :]
    ext_ref[0:SUBLANES, :] = ext_ref[tb:tb + SUBLANES, :]
    p = p + mu_ref[...] * (prev - p)
    r, k, v = p[:, 0:WIDTH], p[:, WIDTH:2 * WIDTH], p[:, 2 * WIDTH:3 * WIDTH]
    wd = p[:, 3 * WIDTH:3 * WIDTH + LORA]
    ad = p[:, 3 * WIDTH + LORA:3 * WIDTH + 2 * LORA]
    gate = p[:, 3 * WIDTH + 2 * LORA:]

    w_raw = -jax.nn.softplus(-(w0_ref[...] + _dot(jnp.tanh(wd), wup_ref[...]))) - 0.5
    logw = -jnp.exp(w_raw)
    a = jax.nn.sigmoid(a0_ref[...] + _dot(ad, aup_ref[...]))

    red = red_ref[...]
    expd = expd_ref[...]
    kkp = k * kk_ref[...]
    kk = kkp * _dot_sel(lax.rsqrt(_dot_sel(kkp * kkp, red, SUM_PASSES) + L2_EPS), expd, EXPAND_PASSES)
    k2 = k * (1.0 + (a - 1.0) * ka_ref[...])
    pa = kk * a

    b = _sel_dot(cum_ref[...], logw, CUMSUM_PASSES)
    enb = jnp.exp(-b)
    qt_ref[...] = -kk * jnp.exp(b - logw)
    rt_ref[...] = r * jnp.exp(b)
    ph_ref[...] = pa * enb
    kh_ref[...] = k2 * enb
    pa_ref[...] = pa
    k2_ref[...] = k2
    v_ref[...] = v
    b_ref[...] = b

    mk = _pair_masks()
    lo, causal, strict, bd, eye = mk["lo"], mk["causal"], mk["strict"], mk["bd"], mk["eye"]
    tiles = _pair_tiles()

    def prep_body(i, carry):
        cs = [i * PREP_CHUNKS + j for j in range(PREP_CHUNKS)]
        rws = [pl.ds(pl.multiple_of(c * CHUNK, CHUNK), CHUNK) for c in cs]
        bl = [b_ref[pl.ds(pl.multiple_of(c * CHUNK, CHUNK) + CHUNK - 1, 1), :] for c in cs]
        ebl = [jnp.exp(b1 - b_ref[rows, :]) for b1, rows in zip(bl, rws)]
        pt_c = [pa_ref[rows, :] * e for rows, e in zip(rws, ebl)]
        kt_c = [k2_ref[rows, :] * e for rows, e in zip(rws, ebl)]
        wl = [jnp.exp(b1) for b1 in bl]
        items = [(j, p) for j in range(PREP_CHUNKS) for p in range(PAIRS)]
        qt = [qt_ref[rws[j], tiles[p]] for j, p in items]
        rt = [rt_ref[rws[j], tiles[p]] for j, p in items]
        v_t = [v_ref[rws[j], tiles[p]] for j, p in items]
        qr = [jnp.concatenate([a, b], axis=0) for a, b in zip(qt, rt)]
        aap = [_dot_nt(t, _stack(ph_ref[rws[j], tiles[p]], lo)) for (j, p), t in zip(items, qr)]
        aak = [_dot_nt(t, _stack(kh_ref[rws[j], tiles[p]], lo)) for (j, p), t in zip(items, qr)]
        av = [_dot(jnp.concatenate([jnp.where(strict, t[0:CHUNK], 0.0), jnp.where(causal, t[CHUNK:], 0.0)],
                                   axis=0), _stack(v, lo)) for t, v in zip(aak, v_t)]
        tinv = _bd_inverse_all([_stack(jnp.where(strict, t[0:CHUNK], 0.0), lo) for t in aap], eye)
        qu = [_dot(t, jnp.concatenate([_stack(a, lo), _stack(b[0:CHUNK], lo)], axis=1))
              for t, a, b in zip(tinv, qt, av)]
        t1 = [_dot(jnp.where(causal, t[CHUNK:], 0.0), y) for t, y in zip(aap, qu)]
        qu_t = [y[0:CHUNK] + y[CHUNK:] for y in qu]
        t2 = [_dot_tn(pt_c[j][:, tiles[p]], y[:, 0:LANES]) for (j, p), y in zip(items, qu_t)]
        t3 = [_dot_tn(jnp.concatenate([pt_c[j][:, tiles[p]], kt_c[j][:, tiles[p]]], axis=0),
                      jnp.concatenate([y[:, LANES:], v], axis=0))
              for (j, p), y, v in zip(items, qu_t, v_t)]
        for n, (j, p) in enumerate(items):
            qm_ref[cs[j], p, 0:CHUNK, :] = rt[n] + t1[n][:, 0:LANES]
            qm_ref[cs[j], p, CHUNK:, :] = eye * wl[j][:, tiles[p]] + jnp.where(bd, t2[n], 0.0)
            n_ref[cs[j], p] = jnp.where(bd, t3[n], 0.0)
            o_ref[rws[j], tiles[p]] = t1[n][:, LANES:] + av[n][CHUNK:]
        return carry

    lax.fori_loop(0, tb // (CHUNK * PREP_CHUNKS), prep_body, 0)

    def scan_body(c, carry):
        rows = pl.ds(pl.multiple_of(c * CHUNK, CHUNK), CHUNK)
        out = [_dot(qm_ref[c, p], s_ref[p]) for p in range(PAIRS)]
        for p in range(PAIRS):
            s_ref[p] = out[p][CHUNK:] + n_ref[c, p]
            o_ref[rows, tiles[p]] = o_ref[rows, tiles[p]] + out[p][0:CHUNK]
        return carry

    lax.fori_loop(0, tb // CHUNK, scan_body, 0)

    o = o_ref[...]
    inv_d = 1.0 / HEAD_DIM
    mean = _dot_sel(_dot_sel(o, red, SUM_PASSES) * inv_d, expd, EXPAND_PASSES)
    d = o - mean
    var = _dot_sel(d * d, red, SUM_PASSES) * inv_d
    on = d * _dot_sel(lax.rsqrt(var + RWKV_GN_EPS), expd, EXPAND_PASSES)
    on = on * lnw_ref[...] + lnb_ref[...]
    bonus = _dot_sel(_dot_sel(r * k2 * rk_ref[...], red, SUM_PASSES), expd, EXPAND_PASSES) * v
    y_ref[...] = (on + bonus) * _silu(gate)


def _hgrn_kernel(x_ref, w_ref, nw_ref, lbraw_ref, hnw_ref, cum_ref, lvl_ref, red_ref, expd_ref, y_ref,
                 s_ref, kin_ref, v_ref, b_ref, qe_ref, zl_ref, o_ref, *, layer, depth):
    tb = x_ref.shape[0]

    @pl.when(pl.program_id(0) == 0)
    def _():
        s_ref[...] = jnp.zeros_like(s_ref)

    lbs = [lbraw_ref[j:j + 1, :] for j in range(depth)]
    mx = lbs[0]
    for t in lbs[1:]:
        mx = jnp.maximum(mx, t)
    es = [jnp.exp(t - mx) for t in lbs]
    tot = es[0]
    for t in es[1:]:
        tot = tot + t
    soft = [t / tot for t in es]
    lb = soft[0]
    for t in soft[1:layer + 1]:
        lb = lb + t
    lb = lb - soft[0]

    p = _norm_proj(x_ref, nw_ref, w_ref)
    q = _silu(p[:, 0:WIDTH])
    f = p[:, WIDTH:2 * WIDTH]
    v = p[:, 2 * WIDTH:3 * WIDTH]
    gate = p[:, 3 * WIDTH:]
    log_g = jax.nn.log_sigmoid(f) + jnp.log1p(lb * jnp.exp(-f))
    kin = (1.0 - lb) * jax.nn.sigmoid(-f)

    b = _sel_dot(cum_ref[...], log_g, CUMSUM_PASSES)
    kin_ref[...] = kin
    v_ref[...] = v
    b_ref[...] = b
    qe_ref[...] = q * jnp.exp(b)
    tr = _iota((tb, WIDTH), 0)
    for l in range(LEVELS):
        upper = (tr & (1 << l)) != 0
        s = 1 << l
        if s == 1:
            d = jnp.where(upper, log_g, 0.0)
        elif 2 * s >= SUBLANES:
            b3 = b.reshape(tb // (2 * s), 2 * s, WIDTH)
            mid = jnp.broadcast_to(b3[:, s - 1:s, :], b3.shape).reshape(tb, WIDTH)
            d = jnp.where(upper, b - mid, mid - b)
        else:
            d = _sel_dot(lvl_ref[MATMUL_LEVELS.index(l)], log_g, CUMSUM_PASSES)
        zl_ref[l] = jnp.where(upper, q, kin) * jnp.exp(d)

    mk = _pair_masks()
    lo, bd = mk["lo"], mk["bd"]
    kj = _iota((LANES, CHUNK), 0) & (CHUNK - 1)
    qi = _iota((LANES, CHUNK), 1)
    lvl_masks_t = [((qi >> (l + 1)) == (kj >> (l + 1))) & ((qi & (1 << l)) != 0) & ((kj & (1 << l)) == 0)
                   for l in range(LEVELS)]
    tiles = _pair_tiles()

    def chunk_body(i, carry):
        cs = [i * HGRN_CHUNKS + j for j in range(HGRN_CHUNKS)]
        rws = [pl.ds(pl.multiple_of(c * CHUNK, CHUNK), CHUNK) for c in cs]
        bl = [b_ref[pl.ds(pl.multiple_of(c * CHUNK, CHUNK) + CHUNK - 1, 1), :] for c in cs]
        kt_c = [kin_ref[rows, :] * jnp.exp(b1 - b_ref[rows, :]) for b1, rows in zip(bl, rws)]
        wl = [jnp.exp(b1) for b1 in bl]
        st = [s_ref[p] for p in range(PAIRS)]
        for j in range(HGRN_CHUNKS):
            att = [None] * PAIRS
            for l in range(LEVELS):
                for p in range(PAIRS):
                    z = zl_ref[l, rws[j], tiles[p]]
                    t = jnp.where(lvl_masks_t[l], _dot_nt(_stack(z, lo), z), 0.0)
                    att[p] = t if att[p] is None else att[p] + t
            v_t = [v_ref[rws[j], tiles[p]] for p in range(PAIRS)]
            oi = [_dot_tn(a, _stack(v, lo)) for a, v in zip(att, v_t)]
            sn = [jnp.where(bd, _dot_tn(v, kt_c[j][:, tiles[p]]), 0.0) for p, v in enumerate(v_t)]
            for p in range(PAIRS):
                o_ref[rws[j], tiles[p]] = oi[p] + _dot_nt(qe_ref[rws[j], tiles[p]], st[p])
                st[p] = st[p] * wl[j][:, tiles[p]] + sn[p]
        for p in range(PAIRS):
            s_ref[p] = st[p]
        return carry

    lax.fori_loop(0, tb // (CHUNK * HGRN_CHUNKS), chunk_body, 0)

    o = o_ref[...] + _dot_sel(_dot_sel(q * kin, red_ref[...], SUM_PASSES), expd_ref[...], EXPAND_PASSES) * v
    ms = jnp.mean(o * o, axis=-1, keepdims=True)
    y_ref[...] = (o * lax.rsqrt(ms + NORM_EPS)) * hnw_ref[...] * _silu(gate)


def _outproj_kernel(x_ref, ya_ref, yb_ref, yc_ref, w_ref, fnw_ref, o_ref, *, final):
    acc = x_ref[...]
    for i, y_ref in enumerate((ya_ref, yb_ref, yc_ref)):
        acc = acc + _dot(y_ref[...].astype(BF16), w_ref[i * WIDTH:(i + 1) * WIDTH, :])
    if final:
        ms = jnp.mean(acc * acc, axis=-1, keepdims=True)
        acc = (acc * lax.rsqrt(ms + NORM_EPS)) * fnw_ref[...]
    o_ref[...] = acc


def _full(shape):
    nd = len(shape)
    return pl.BlockSpec(shape, lambda i, _nd=nd: (0,) * _nd)


def _rows(tb, width):
    return pl.BlockSpec((tb, width), lambda i: (i, 0))


def _mixer_call(body, name, x2, w, layer, cols, params, scratch):
    t, d = x2.shape
    tb = TIME_BLOCK
    w_spec = pl.BlockSpec((None, d, cols), lambda i: (layer, 0, 0))
    return pl.pallas_call(
        body,
        grid=(t // tb,),
        in_specs=[_rows(tb, d), w_spec] + [_full(a.shape) for a in params],
        out_specs=_rows(tb, WIDTH),
        out_shape=jax.ShapeDtypeStruct((t, WIDTH), F32),
        scratch_shapes=scratch,
        compiler_params=pltpu.CompilerParams(dimension_semantics=("arbitrary",),
                                             vmem_limit_bytes=VMEM_LIMIT),
        name=name,
    )(x2, w, *params)


def _slab(tb):
    return pltpu.VMEM((tb, WIDTH), F32)


def _state():
    return pltpu.VMEM((PAIRS, LANES, LANES), F32)


def _folded(nc):
    return [pltpu.VMEM((nc, PAIRS, CHUNK + LANES, LANES), F32), pltpu.VMEM((nc, PAIRS, LANES, LANES), F32)]


def _row(a):
    return a.reshape(1, -1).astype(F32)


def _lane_pad(a, offset):
    return jnp.zeros((1, LANES), F32).at[0, offset:offset + HEADS].set(a.astype(F32))


def kernel(x, norm_w, w_in, gdn_conv_w, gdn_a_log, gdn_dt_bias, gdn_norm_w, rwkv_mu, rwkv_w0, rwkv_w_up, rwkv_a0, rwkv_a_up, rwkv_k_k, rwkv_k_a, rwkv_r_k, rwkv_ln_w, rwkv_ln_b, hgrn_lower_bounds, hgrn_norm_w, w_out, final_norm_w):
    bsz, seq, d = x.shape
    depth = norm_w.shape[0]
    assert bsz == 1 and seq % TIME_BLOCK == 0 and seq % OUT_BLOCK == 0
    tb = TIME_BLOCK
    nc = tb // CHUNK
    x2 = x.reshape(seq, d)

    gdn_cols = 3 * WIDTH + 2 * HEADS + WIDTH
    rwkv_off = gdn_cols
    hgrn_off = gdn_cols + RWKV_PROJ

    cum = _const(_np_chunk_cumsum(CONST_ROWS))
    red = _const(_np_head_reduce())
    exp0 = _const(_np_head_expand(0))
    exp8 = _const(_np_head_expand(HEADS))
    lvl = _const(np.stack([_np_level_matrix(CONST_ROWS, 1 << l) for l in MATMUL_LEVELS]))

    wb = w_in.astype(BF16)
    w_rwkv = wb[:, :, rwkv_off:rwkv_off + RWKV_PROJ]
    w_hgrn = wb[:, :, hgrn_off:hgrn_off + HGRN_PROJ]
    w_out_b = w_out.astype(BF16)

    for l in range(depth):
        nw = _row(norm_w[l])

        y_a = _mixer_call(
            _gdn_kernel, f"gdn_{l}", x2, wb, l, PROJ_BLOCK,
            [nw, gdn_conv_w[l].astype(F32), _lane_pad(gdn_a_log[l], HEADS),
             _lane_pad(gdn_dt_bias[l], HEADS), _row(jnp.tile(gdn_norm_w[l], HEADS)),
             cum, red, exp0, exp8],
            [_state(), pltpu.VMEM((tb + SUBLANES, 3 * WIDTH), F32)] + [_slab(tb)] * 7
            + [pltpu.VMEM((tb, LANES), F32)] + _folded(nc) + [_slab(tb)])

        y_b = _mixer_call(
            _rwkv_kernel, f"rwkv_{l}", x2, w_rwkv, l, RWKV_PROJ,
            [nw, _row(rwkv_mu[l]), _row(rwkv_w0[l]), rwkv_w_up[l].astype(F32),
             _row(rwkv_a0[l]), rwkv_a_up[l].astype(F32), _row(rwkv_k_k[l]), _row(rwkv_k_a[l]),
             _row(rwkv_r_k[l]), _row(rwkv_ln_w[l]), _row(rwkv_ln_b[l]), cum, red, exp0],
            [_state(), pltpu.VMEM((tb + SUBLANES, RWKV_PROJ), F32)] + [_slab(tb)] * 8
            + _folded(nc) + [_slab(tb)])

        y_c = _mixer_call(
            functools.partial(_hgrn_kernel, layer=l, depth=depth), f"hgrn_{l}", x2, w_hgrn, l, HGRN_PROJ,
            [nw, hgrn_lower_bounds.astype(F32), _row(hgrn_norm_w[l]), cum, lvl, red, exp0],
            [_state()] + [_slab(tb)] * 4 + [pltpu.VMEM((LEVELS, tb, WIDTH), F32), _slab(tb)])

        final = l == depth - 1
        x2 = pl.pallas_call(
            functools.partial(_outproj_kernel, final=final),
            grid=(seq // OUT_BLOCK,),
            in_specs=[_rows(OUT_BLOCK, d)] + [_rows(OUT_BLOCK, WIDTH)] * 3
            + [_full((3 * WIDTH, d)), _full((1, d))],
            out_specs=_rows(OUT_BLOCK, d),
            out_shape=jax.ShapeDtypeStruct((seq, d), F32),
            compiler_params=pltpu.CompilerParams(dimension_semantics=("arbitrary",),
                                                 vmem_limit_bytes=VMEM_LIMIT),
            name=f"outproj_{l}",
        )(x2, y_a, y_b, y_c, w_out_b[l], _row(final_norm_w))

    return x2.reshape(bsz, seq, d)
```

```python
import functools

import numpy as np
import jax
import jax.numpy as jnp
from jax import lax
from jax.experimental import pallas as pl
from jax.experimental.pallas import tpu as pltpu

F32 = jnp.float32
BF16 = jnp.bfloat16

HEAD_DIM = 64
HEADS = 8
PAIRS = HEADS // 2
WIDTH = HEADS * HEAD_DIM
CHUNK = 64
LEVELS = CHUNK.bit_length() - 1
CONV_WIDTH = 4
LORA = 64
NORM_EPS = 1e-6
L2_EPS = 1e-6
RWKV_GN_EPS = 64e-5
LANES = 128
SUBLANES = 8
MATMUL_LEVELS = [l for l in range(1, LEVELS) if 2 * (1 << l) < SUBLANES]
RWKV_PROJ = 4 * WIDTH + 2 * LORA
HGRN_PROJ = 4 * WIDTH
PROJ_BLOCK = 4 * WIDTH + LANES
TIME_BLOCK = 512
CONST_ROWS = 256
PREP_CHUNKS = 8
HGRN_CHUNKS = 4
SUM_PASSES = 1
EXPAND_PASSES = 1
CUMSUM_PASSES = 2
GDN_DECAY_PASSES = 3
OUT_BLOCK = 512
VMEM_LIMIT = 56 * 1024 * 1024


def _dot(a, b):
    return jnp.dot(a, b, preferred_element_type=F32)


def _dot_nt(a, b):
    return lax.dot_general(a, b, (((1,), (1,)), ((), ())), preferred_element_type=F32)


def _dot_tn(a, b):
    return lax.dot_general(a, b, (((0,), (0,)), ((), ())), preferred_element_type=F32)


def _split_bf16(x, n):
    parts, r = [], x
    for i in range(n):
        p = r.astype(BF16)
        parts.append(p)
        if i + 1 < n:
            r = r - p.astype(F32)
    return parts


def _sel_dot(m01, x, n):
    parts = _split_bf16(x, n)
    blocks = []
    for r0 in range(0, x.shape[0], CONST_ROWS):
        out = None
        for p in parts:
            t = _dot(m01, p[r0:r0 + CONST_ROWS])
            out = t if out is None else out + t
        blocks.append(out)
    return blocks[0] if len(blocks) == 1 else jnp.concatenate(blocks, axis=0)


def _dot_sel(x, m01, n):
    out = None
    for p in _split_bf16(x, n):
        t = _dot(p, m01)
        out = t if out is None else out + t
    return out


def _iota(shape, dim):
    return lax.broadcasted_iota(jnp.int32, shape, dim)


def _bd_inverse_all(ps, eye_bd):
    xs = [eye_bd + p for p in ps]
    ps = [_dot(p, p) for p in ps]
    for _ in range(LEVELS - 2):
        r = [_dot(p, jnp.concatenate([x, p], axis=1)) for p, x in zip(ps, xs)]
        xs = [x + t[:, 0:LANES] for x, t in zip(xs, r)]
        ps = [t[:, LANES:] for t in r]
    return [x + _dot(p, x) for p, x in zip(ps, xs)]


def _pair_masks():
    i = _iota((CHUNK, LANES), 0)
    c = _iota((CHUNK, LANES), 1)
    cm = c & (HEAD_DIM - 1)
    r = _iota((LANES, LANES), 0)
    l = _iota((LANES, LANES), 1)
    return dict(lo=c < HEAD_DIM, causal=cm <= i, strict=cm < i,
                bd=(r // HEAD_DIM) == (l // HEAD_DIM), eye=jnp.where(r == l, 1.0, 0.0))


def _stack(x, lo):
    return jnp.concatenate([jnp.where(lo, x, 0.0), jnp.where(lo, 0.0, x)], axis=0)


def _pair_tiles():
    return [slice(p * LANES, (p + 1) * LANES) for p in range(PAIRS)]


def _silu(x):
    return x * jax.nn.sigmoid(x)


def _norm_proj(x_ref, nw_ref, w_ref):
    x = x_ref[...]
    ms = jnp.mean(x * x, axis=-1, keepdims=True)
    h = (x * lax.rsqrt(ms + NORM_EPS)) * nw_ref[...]
    return _dot(h.astype(BF16), w_ref[...])


def _np_head_expand(offset):
    r = np.arange(LANES)[:, None]
    c = np.arange(WIDTH)[None, :]
    return r == c // HEAD_DIM + offset


def _np_head_reduce():
    return _np_head_expand(0).T


def _np_chunk_cumsum(n):
    r = np.arange(n)[:, None]
    c = np.arange(n)[None, :]
    return (r // CHUNK == c // CHUNK) & (c <= r)


def _np_level_matrix(n, s):
    r = np.arange(n)[:, None]
    c = np.arange(n)[None, :]
    tr, tc = r % CHUNK, c % CHUNK
    mid = (tr // (2 * s)) * (2 * s) + s - 1
    upper = (tr // s) % 2 == 1
    m = (upper & (tc > mid) & (tc <= tr)) | (~upper & (tc > tr) & (tc <= mid))
    return (r // CHUNK == c // CHUNK) & m


def _const(mask):
    return jnp.asarray(np.asarray(mask, np.float32), BF16)


def _gdn_kernel(x_ref, w_ref, nw_ref, conv_ref, alog_ref, dtb_ref, gnw_ref, cum_ref, red_ref, expb_ref,
                expg_ref, y_ref,
                s_ref, ext_ref, q_ref, k_ref, kb_ref, vb_ref, kbe_ref, qd_ref, gcf_ref, gcs_ref,
                qm_ref, n_ref, o_ref):
    tb = x_ref.shape[0]

    @pl.when(pl.program_id(0) == 0)
    def _():
        s_ref[...] = jnp.zeros_like(s_ref)
        ext_ref[0:SUBLANES, :] = jnp.zeros((SUBLANES, 3 * WIDTH), F32)

    p = _norm_proj(x_ref, nw_ref, w_ref)
    gate = p[:, 3 * WIDTH + 2 * HEADS:4 * WIDTH + 2 * HEADS]
    small = p[:, 3 * WIDTH:3 * WIDTH + LANES]

    ext_ref[SUBLANES:SUBLANES + tb, :] = p[:, 0:3 * WIDTH]
    conv = None
    for j in range(CONV_WIDTH):
        t = conv_ref[j:j + 1, :] * ext_ref[pl.ds(SUBLANES - (CONV_WIDTH - 1) + j, tb), :]
        conv = t if conv is None else conv + t
    ext_ref[0:SUBLANES, :] = ext_ref[tb:tb + SUBLANES, :]
    qkv = _silu(conv)
    q, k, v = qkv[:, 0:WIDTH], qkv[:, WIDTH:2 * WIDTH], qkv[:, 2 * WIDTH:3 * WIDTH]

    red = red_ref[...]
    exp_b = expb_ref[...]
    q_rs = lax.rsqrt(_dot_sel(q * q, red, SUM_PASSES) + L2_EPS)
    k_rs = lax.rsqrt(_dot_sel(k * k, red, SUM_PASSES) + L2_EPS)
    qn = (q * _dot_sel(q_rs, exp_b, EXPAND_PASSES)) * (HEAD_DIM ** -0.5)
    kn = k * _dot_sel(k_rs, exp_b, EXPAND_PASSES)

    beta = _dot_sel(jax.nn.sigmoid(small), exp_b, EXPAND_PASSES)
    g_small = -jnp.exp(alog_ref[...]) * jax.nn.softplus(small + dtb_ref[...])
    gc_small = _sel_dot(cum_ref[...], g_small, GDN_DECAY_PASSES)
    gc_full = _dot_sel(gc_small, expg_ref[...], GDN_DECAY_PASSES)
    egc = jnp.exp(gc_full)
    kb = kn * beta

    q_ref[...] = qn
    k_ref[...] = kn
    kb_ref[...] = kb
    vb_ref[...] = v * beta
    kbe_ref[...] = kb * egc
    qd_ref[...] = qn * egc
    gcf_ref[...] = gc_full
    gcs_ref[...] = gc_small

    mk = _pair_masks()
    lo, causal, strict, bd, eye = mk["lo"], mk["causal"], mk["strict"], mk["bd"], mk["eye"]
    tiles = _pair_tiles()

    def prep_body(i, carry):
        cs = [i * PREP_CHUNKS + j for j in range(PREP_CHUNKS)]
        rws = [pl.ds(pl.multiple_of(c * CHUNK, CHUNK), CHUNK) for c in cs]
        gc_c = [gcf_ref[rows, :] for rows in rws]
        gl = [gcf_ref[pl.ds(pl.multiple_of(c * CHUNK, CHUNK) + CHUNK - 1, 1), :] for c in cs]
        kd_c = [k_ref[rows, :] * jnp.exp(g1 - g2) for rows, g1, g2 in zip(rws, gl, gc_c)]
        last = [jnp.exp(g1) for g1 in gl]
        g_rows = [gcs_ref[rows, :].T for rows in rws]
        items = [(j, p) for j in range(PREP_CHUNKS) for p in range(PAIRS)]
        ak = [_dot_nt(jnp.concatenate([kb_ref[rws[j], tiles[p]], q_ref[rws[j], tiles[p]]], axis=0),
                      _stack(k_ref[rws[j], tiles[p]], lo)) for j, p in items]
        g_row = [jnp.concatenate([g_rows[j][HEADS + 2 * p:HEADS + 2 * p + 1, :],
                                  g_rows[j][HEADS + 2 * p + 1:HEADS + 2 * p + 2, :]], axis=1)
                 for j, p in items]
        dm = [jnp.where(causal, jnp.exp(jnp.where(causal, gc_c[j][:, tiles[p]] - g, 0.0)), 0.0)
              for (j, p), g in zip(items, g_row)]
        qk = [t[CHUNK:] * d for t, d in zip(ak, dm)]
        tm = _bd_inverse_all([_stack(jnp.where(strict, -(t[0:CHUNK] * d), 0.0), lo)
                              for t, d in zip(ak, dm)], eye)
        uw = [_dot(t, jnp.concatenate([_stack(vb_ref[rws[j], tiles[p]], lo),
                                       _stack(kbe_ref[rws[j], tiles[p]], lo)], axis=1))
              for (j, p), t in zip(items, tm)]
        t1 = [_dot(a, b) for a, b in zip(qk, uw)]
        t2 = [_dot_tn(kd_c[j][:, tiles[p]], b[0:CHUNK] + b[CHUNK:])
              for (j, p), b in zip(items, uw)]
        for n, (j, p) in enumerate(items):
            qm_ref[cs[j], p, 0:CHUNK, :] = qd_ref[rws[j], tiles[p]] - t1[n][:, LANES:]
            qm_ref[cs[j], p, CHUNK:, :] = eye * last[j][:, tiles[p]] - jnp.where(bd, t2[n][:, LANES:], 0.0)
            n_ref[cs[j], p] = jnp.where(bd, t2[n][:, 0:LANES], 0.0)
            o_ref[rws[j], tiles[p]] = t1[n][:, 0:LANES]
        return carry

    lax.fori_loop(0, tb // (CHUNK * PREP_CHUNKS), prep_body, 0)

    def scan_body(c, carry):
        rows = pl.ds(pl.multiple_of(c * CHUNK, CHUNK), CHUNK)
        out = [_dot(qm_ref[c, p], s_ref[p]) for p in range(PAIRS)]
        for p in range(PAIRS):
            s_ref[p] = out[p][CHUNK:] + n_ref[c, p]
            o_ref[rows, tiles[p]] = o_ref[rows, tiles[p]] + out[p][0:CHUNK]
        return carry

    lax.fori_loop(0, tb // CHUNK, scan_body, 0)

    o = o_ref[...]
    rs = lax.rsqrt(_dot_sel(o * o, red_ref[...], SUM_PASSES) * (1.0 / HEAD_DIM) + NORM_EPS)
    y_ref[...] = (o * _dot_sel(rs, expb_ref[...], EXPAND_PASSES)) * gnw_ref[...] * _silu(gate)


def _rwkv_kernel(x_ref, w_ref, nw_ref, mu_ref, w0_ref, wup_ref, a0_ref, aup_ref, kk_ref, ka_ref,
                 rk_ref, lnw_ref, lnb_ref, cum_ref, red_ref, expd_ref, y_ref,
                 s_ref, ext_ref, qt_ref, rt_ref, ph_ref, kh_ref, pa_ref, k2_ref, v_ref, b_ref,
                 qm_ref, n_ref, o_ref):
    tb = x_ref.shape[0]

    @pl.when(pl.program_id(0) == 0)
    def _():
        s_ref[...] = jnp.zeros_like(s_ref)
        ext_ref[0:SUBLANES, :] = jnp.zeros((SUBLANES, RWKV_PROJ), F32)

    p = _norm_proj(x_ref, nw_ref, w_ref)
    ext_ref[SUBLANES:SUBLANES + tb, :] = p
    prev = ext_ref[pl.ds(SUBLANES - 1, tb), :]
    ext_ref[0:SUBLANES, :] = ext_ref[tb:tb + SUBLANES, :]
    p = p + mu_ref[...] * (prev - p)
    r, k, v = p[:, 0:WIDTH], p[:, WIDTH:2 * WIDTH], p[:, 2 * WIDTH:3 * WIDTH]
    wd = p[:, 3 * WIDTH:3 * WIDTH + LORA]
    ad = p[:, 3 * WIDTH + LORA:3 * WIDTH + 2 * LORA]
    gate = p[:, 3 * WIDTH + 2 * LORA:]

    w_raw = -jax.nn.softplus(-(w0_ref[...] + _dot(jnp.tanh(wd), wup_ref[...]))) - 0.5
    logw = -jnp.exp(w_raw)
    a = jax.nn.sigmoid(a0_ref[...] + _dot(ad, aup_ref[...]))

    red = red_ref[...]
    expd = expd_ref[...]
    kkp = k * kk_ref[...]
    kk = kkp * _dot_sel(lax.rsqrt(_dot_sel(kkp * kkp, red, SUM_PASSES) + L2_EPS), expd, EXPAND_PASSES)
    k2 = k * (1.0 + (a - 1.0) * ka_ref[...])
    pa = kk * a

    b = _sel_dot(cum_ref[...], logw, CUMSUM_PASSES)
    enb = jnp.exp(-b)
    qt_ref[...] = -kk * jnp.exp(b - logw)
    rt_ref[...] = r * jnp.exp(b)
    ph_ref[...] = pa * enb
    kh_ref[...] = k2 * enb
    pa_ref[...] = pa
    k2_ref[...] = k2
    v_ref[...] = v
    b_ref[...] = b

    mk = _pair_masks()
    lo, causal, strict, bd, eye = mk["lo"], mk["causal"], mk["strict"], mk["bd"], mk["eye"]
    tiles = _pair_tiles()

    def prep_body(i, carry):
        cs = [i * PREP_CHUNKS + j for j in range(PREP_CHUNKS)]
        rws = [pl.ds(pl.multiple_of(c * CHUNK, CHUNK), CHUNK) for c in cs]
        bl = [b_ref[pl.ds(pl.multiple_of(c * CHUNK, CHUNK) + CHUNK - 1, 1), :] for c in cs]
        ebl = [jnp.exp(b1 - b_ref[rows, :]) for b1, rows in zip(bl, rws)]
        pt_c = [pa_ref[rows, :] * e for rows, e in zip(rws, ebl)]
        kt_c = [k2_ref[rows, :] * e for rows, e in zip(rws, ebl)]
        wl = [jnp.exp(b1) for b1 in bl]
        items = [(j, p) for j in range(PREP_CHUNKS) for p in range(PAIRS)]
        qt = [qt_ref[rws[j], tiles[p]] for j, p in items]
        rt = [rt_ref[rws[j], tiles[p]] for j, p in items]
        v_t = [v_ref[rws[j], tiles[p]] for j, p in items]
        qr = [jnp.concatenate([a, b], axis=0) for a, b in zip(qt, rt)]
        aap = [_dot_nt(t, _stack(ph_ref[rws[j], tiles[p]], lo)) for (j, p), t in zip(items, qr)]
        aak = [_dot_nt(t, _stack(kh_ref[rws[j], tiles[p]], lo)) for (j, p), t in zip(items, qr)]
        av = [_dot(jnp.concatenate([jnp.where(strict, t[0:CHUNK], 0.0), jnp.where(causal, t[CHUNK:], 0.0)],
                                   axis=0), _stack(v, lo)) for t, v in zip(aak, v_t)]
        tinv = _bd_inverse_all([_stack(jnp.where(strict, t[0:CHUNK], 0.0), lo) for t in aap], eye)
        qu = [_dot(t, jnp.concatenate([_stack(a, lo), _stack(b[0:CHUNK], lo)], axis=1))
              for t, a, b in zip(tinv, qt, av)]
        t1 = [_dot(jnp.where(causal, t[CHUNK:], 0.0), y) for t, y in zip(aap, qu)]
        qu_t = [y[0:CHUNK] + y[CHUNK:] for y in qu]
        t2 = [_dot_tn(pt_c[j][:, tiles[p]], y[:, 0:LANES]) for (j, p), y in zip(items, qu_t)]
        t3 = [_dot_tn(jnp.concatenate([pt_c[j][:, tiles[p]], kt_c[j][:, tiles[p]]], axis=0),
                      jnp.concatenate([y[:, LANES:], v], axis=0))
              for (j, p), y, v in zip(items, qu_t, v_t)]
        for n, (j, p) in enumerate(items):
            qm_ref[cs[j], p, 0:CHUNK, :] = rt[n] + t1[n][:, 0:LANES]
            qm_ref[cs[j], p, CHUNK:, :] = eye * wl[j][:, tiles[p]] + jnp.where(bd, t2[n], 0.0)
            n_ref[cs[j], p] = jnp.where(bd, t3[n], 0.0)
            o_ref[rws[j], tiles[p]] = t1[n][:, LANES:] + av[n][CHUNK:]
        return carry

    lax.fori_loop(0, tb // (CHUNK * PREP_CHUNKS), prep_body, 0)

    def scan_body(c, carry):
        rows = pl.ds(pl.multiple_of(c * CHUNK, CHUNK), CHUNK)
        out = [_dot(qm_ref[c, p], s_ref[p]) for p in range(PAIRS)]
        for p in range(PAIRS):
            s_ref[p] = out[p][CHUNK:] + n_ref[c, p]
            o_ref[rows, tiles[p]] = o_ref[rows, tiles[p]] + out[p][0:CHUNK]
        return carry

    lax.fori_loop(0, tb // CHUNK, scan_body, 0)

    o = o_ref[...]
    inv_d = 1.0 / HEAD_DIM
    mean = _dot_sel(_dot_sel(o, red, SUM_PASSES) * inv_d, expd, EXPAND_PASSES)
    d = o - mean
    var = _dot_sel(d * d, red, SUM_PASSES) * inv_d
    on = d * _dot_sel(lax.rsqrt(var + RWKV_GN_EPS), expd, EXPAND_PASSES)
    on = on * lnw_ref[...] + lnb_ref[...]
    bonus = _dot_sel(_dot_sel(r * k2 * rk_ref[...], red, SUM_PASSES), expd, EXPAND_PASSES) * v
    y_ref[...] = (on + bonus) * _silu(gate)


def _hgrn_kernel(x_ref, w_ref, nw_ref, lbraw_ref, hnw_ref, cum_ref, lvl_ref, red_ref, expd_ref, y_ref,
                 s_ref, kin_ref, v_ref, b_ref, qe_ref, zl_ref, o_ref, *, layer, depth):
    tb = x_ref.shape[0]

    @pl.when(pl.program_id(0) == 0)
    def _():
        s_ref[...] = jnp.zeros_like(s_ref)

    lbs = [lbraw_ref[j:j + 1, :] for j in range(depth)]
    mx = lbs[0]
    for t in lbs[1:]:
        mx = jnp.maximum(mx, t)
    es = [jnp.exp(t - mx) for t in lbs]
    tot = es[0]
    for t in es[1:]:
        tot = tot + t
    soft = [t / tot for t in es]
    lb = soft[0]
    for t in soft[1:layer + 1]:
        lb = lb + t
    lb = lb - soft[0]

    p = _norm_proj(x_ref, nw_ref, w_ref)
    q = _silu(p[:, 0:WIDTH])
    f = p[:, WIDTH:2 * WIDTH]
    v = p[:, 2 * WIDTH:3 * WIDTH]
    gate = p[:, 3 * WIDTH:]
    log_g = jax.nn.log_sigmoid(f) + jnp.log1p(lb * jnp.exp(-f))
    kin = (1.0 - lb) * jax.nn.sigmoid(-f)

    b = _sel_dot(cum_ref[...], log_g, CUMSUM_PASSES)
    kin_ref[...] = kin
    v_ref[...] = v
    b_ref[...] = b
    qe_ref[...] = q * jnp.exp(b)
    tr = _iota((tb, WIDTH), 0)
    for l in range(LEVELS):
        upper = (tr & (1 << l)) != 0
        s = 1 << l
        if s == 1:
            d = jnp.where(upper, log_g, 0.0)
        elif 2 * s >= SUBLANES:
            b3 = b.reshape(tb // (2 * s), 2 * s, WIDTH)
            mid = jnp.broadcast_to(b3[:, s - 1:s, :], b3.shape).reshape(tb, WIDTH)
            d = jnp.where(upper, b - mid, mid - b)
        else:
            d = _sel_dot(lvl_ref[MATMUL_LEVELS.index(l)], log_g, CUMSUM_PASSES)
        zl_ref[l] = jnp.where(upper, q, kin) * jnp.exp(d)

    mk = _pair_masks()
    lo, bd = mk["lo"], mk["bd"]
    kj = _iota((LANES, CHUNK), 0) & (CHUNK - 1)
    qi = _iota((LANES, CHUNK), 1)
    lvl_masks_t = [((qi >> (l + 1)) == (kj >> (l + 1))) & ((qi & (1 << l)) != 0) & ((kj & (1 << l)) == 0)
                   for l in range(LEVELS)]
    tiles = _pair_tiles()

    def chunk_body(i, carry):
        cs = [i * HGRN_CHUNKS + j for j in range(HGRN_CHUNKS)]
        rws = [pl.ds(pl.multiple_of(c * CHUNK, CHUNK), CHUNK) for c in cs]
        bl = [b_ref[pl.ds(pl.multiple_of(c * CHUNK, CHUNK) + CHUNK - 1, 1), :] for c in cs]
        kt_c = [kin_ref[rows, :] * jnp.exp(b1 - b_ref[rows, :]) for b1, rows in zip(bl, rws)]
        wl = [jnp.exp(b1) for b1 in bl]
        st = [s_ref[p] for p in range(PAIRS)]
        for j in range(HGRN_CHUNKS):
            att = [None] * PAIRS
            for l in range(LEVELS):
                for p in range(PAIRS):
                    z = zl_ref[l, rws[j], tiles[p]]
                    t = jnp.where(lvl_masks_t[l], _dot_nt(_stack(z, lo), z), 0.0)
                    att[p] = t if att[p] is None else att[p] + t
            v_t = [v_ref[rws[j], tiles[p]] for p in range(PAIRS)]
            oi = [_dot_tn(a, _stack(v, lo)) for a, v in zip(att, v_t)]
            sn = [jnp.where(bd, _dot_tn(v, kt_c[j][:, tiles[p]]), 0.0) for p, v in enumerate(v_t)]
            for p in range(PAIRS):
                o_ref[rws[j], tiles[p]] = oi[p] + _dot_nt(qe_ref[rws[j], tiles[p]], st[p])
                st[p] = st[p] * wl[j][:, tiles[p]] + sn[p]
        for p in range(PAIRS):
            s_ref[p] = st[p]
        return carry

    lax.fori_loop(0, tb // (CHUNK * HGRN_CHUNKS), chunk_body, 0)

    o = o_ref[...] + _dot_sel(_dot_sel(q * kin, red_ref[...], SUM_PASSES), expd_ref[...], EXPAND_PASSES) * v
    ms = jnp.mean(o * o, axis=-1, keepdims=True)
    y_ref[...] = (o * lax.rsqrt(ms + NORM_EPS)) * hnw_ref[...] * _silu(gate)


def _outproj_kernel(x_ref, ya_ref, yb_ref, yc_ref, w_ref, fnw_ref, o_ref, *, final):
    acc = x_ref[...]
    for i, y_ref in enumerate((ya_ref, yb_ref, yc_ref)):
        acc = acc + _dot(y_ref[...].astype(BF16), w_ref[i * WIDTH:(i + 1) * WIDTH, :])
    if final:
        ms = jnp.mean(acc * acc, axis=-1, keepdims=True)
        acc = (acc * lax.rsqrt(ms + NORM_EPS)) * fnw_ref[...]
    o_ref[...] = acc


def _full(shape):
    nd = len(shape)
    return pl.BlockSpec(shape, lambda i, _nd=nd: (0,) * _nd)


def _rows(tb, width):
    return pl.BlockSpec((tb, width), lambda i: (i, 0))


def _mixer_call(body, name, x2, w, layer, cols, params, scratch):
    t, d = x2.shape
    tb = TIME_BLOCK
    w_spec = pl.BlockSpec((None, d, cols), lambda i: (layer, 0, 0))
    return pl.pallas_call(
        body,
        grid=(t // tb,),
        in_specs=[_rows(tb, d), w_spec] + [_full(a.shape) for a in params],
        out_specs=_rows(tb, WIDTH),
        out_shape=jax.ShapeDtypeStruct((t, WIDTH), F32),
        scratch_shapes=scratch,
        compiler_params=pltpu.CompilerParams(dimension_semantics=("arbitrary",),
                                             vmem_limit_bytes=VMEM_LIMIT),
        name=name,
    )(x2, w, *params)


def _slab(tb):
    return pltpu.VMEM((tb, WIDTH), F32)


def _state():
    return pltpu.VMEM((PAIRS, LANES, LANES), F32)


def _folded(nc):
    return [pltpu.VMEM((nc, PAIRS, CHUNK + LANES, LANES), F32), pltpu.VMEM((nc, PAIRS, LANES, LANES), F32)]


def _row(a):
    return a.reshape(1, -1).astype(F32)


def _lane_pad(a, offset):
    return jnp.zeros((1, LANES), F32).at[0, offset:offset + HEADS].set(a.astype(F32))


def kernel(x, norm_w, w_in, gdn_conv_w, gdn_a_log, gdn_dt_bias, gdn_norm_w, rwkv_mu, rwkv_w0, rwkv_w_up, rwkv_a0, rwkv_a_up, rwkv_k_k, rwkv_k_a, rwkv_r_k, rwkv_ln_w, rwkv_ln_b, hgrn_lower_bounds, hgrn_norm_w, w_out, final_norm_w):
    bsz, seq, d = x.shape
    depth = norm_w.shape[0]
    assert bsz == 1 and seq % TIME_BLOCK == 0 and seq % OUT_BLOCK == 0
    tb = TIME_BLOCK
    nc = tb // CHUNK
    x2 = x.reshape(seq, d)

    gdn_cols = 3 * WIDTH + 2 * HEADS + WIDTH
    rwkv_off = gdn_cols
    hgrn_off = gdn_cols + RWKV_PROJ

    cum = _const(_np_chunk_cumsum(CONST_ROWS))
    red = _const(_np_head_reduce())
    exp0 = _const(_np_head_expand(0))
    exp8 = _const(_np_head_expand(HEADS))
    lvl = _const(np.stack([_np_level_matrix(CONST_ROWS, 1 << l) for l in MATMUL_LEVELS]))

    wb = w_in.astype(BF16)
    w_rwkv = wb[:, :, rwkv_off:rwkv_off + RWKV_PROJ]
    w_hgrn = wb[:, :, hgrn_off:hgrn_off + HGRN_PROJ]
    w_out_b = w_out.astype(BF16)

    for l in range(depth):
        nw = _row(norm_w[l])

        y_a = _mixer_call(
            _gdn_kernel, f"gdn_{l}", x2, wb, l, PROJ_BLOCK,
            [nw, gdn_conv_w[l].astype(F32), _lane_pad(gdn_a_log[l], HEADS),
             _lane_pad(gdn_dt_bias[l], HEADS), _row(jnp.tile(gdn_norm_w[l], HEADS)),
             cum, red, exp0, exp8],
            [_state(), pltpu.VMEM((tb + SUBLANES, 3 * WIDTH), F32)] + [_slab(tb)] * 7
            + [pltpu.VMEM((tb, LANES), F32)] + _folded(nc) + [_slab(tb)])

        y_b = _mixer_call(
            _rwkv_kernel, f"rwkv_{l}", x2, w_rwkv, l, RWKV_PROJ,
            [nw, _row(rwkv_mu[l]), _row(rwkv_w0[l]), rwkv_w_up[l].astype(F32),
             _row(rwkv_a0[l]), rwkv_a_up[l].astype(F32), _row(rwkv_k_k[l]), _row(rwkv_k_a[l]),
             _row(rwkv_r_k[l]), _row(rwkv_ln_w[l]), _row(rwkv_ln_b[l]), cum, red, exp0],
            [_state(), pltpu.VMEM((tb + SUBLANES, RWKV_PROJ), F32)] + [_slab(tb)] * 8
            + _folded(nc) + [_slab(tb)])

        y_c = _mixer_call(
            functools.partial(_hgrn_kernel, layer=l, depth=depth), f"hgrn_{l}", x2, w_hgrn, l, HGRN_PROJ,
            [nw, hgrn_lower_bounds.astype(F32), _row(hgrn_norm_w[l]), cum, lvl, red, exp0],
            [_state()] + [_slab(tb)] * 4 + [pltpu.VMEM((LEVELS, tb, WIDTH), F32), _slab(tb)])

        final = l == depth - 1
        x2 = pl.pallas_call(
            functools.partial(_outproj_kernel, final=final),
            grid=(seq // OUT_BLOCK,),
            in_specs=[_rows(OUT_BLOCK, d)] + [_rows(OUT_BLOCK, WIDTH)] * 3
            + [_full((3 * WIDTH, d)), _full((1, d))],
            out_specs=_rows(OUT_BLOCK, d),
            out_shape=jax.ShapeDtypeStruct((seq, d), F32),
            compiler_params=pltpu.CompilerParams(dimension_semantics=("arbitrary",),
                                                 vmem_limit_bytes=VMEM_LIMIT),
            name=f"outproj_{l}",
        )(x2, y_a, y_b, y_c, w_out_b[l], _row(final_norm_w))

    return x2.reshape(bsz, seq, d)
```

```python
import functools

import numpy as np
import jax
import jax.numpy as jnp
from jax import lax
from jax.experimental import pallas as pl
from jax.experimental.pallas import tpu as pltpu

F32 = jnp.float32
BF16 = jnp.bfloat16

HEAD_DIM = 64
HEADS = 8
PAIRS = HEADS // 2
WIDTH = HEADS * HEAD_DIM
CHUNK = 64
LEVELS = CHUNK.bit_length() - 1
CONV_WIDTH = 4
LORA = 64
NORM_EPS = 1e-6
L2_EPS = 1e-6
RWKV_GN_EPS = 64e-5
LANES = 128
SUBLANES = 8
MATMUL_LEVELS = [l for l in range(1, LEVELS) if 2 * (1 << l) < SUBLANES]
RWKV_PROJ = 4 * WIDTH + 2 * LORA
HGRN_PROJ = 4 * WIDTH
PROJ_BLOCK = 4 * WIDTH + LANES
TIME_BLOCK = 512
CONST_ROWS = 256
PREP_CHUNKS = 4
SUM_PASSES = 1
EXPAND_PASSES = 1
CUMSUM_PASSES = 2
GDN_DECAY_PASSES = 3
OUT_BLOCK = 512
VMEM_LIMIT = 56 * 1024 * 1024


def _dot(a, b):
    return jnp.dot(a, b, preferred_element_type=F32)


def _dot_nt(a, b):
    return lax.dot_general(a, b, (((1,), (1,)), ((), ())), preferred_element_type=F32)


def _dot_tn(a, b):
    return lax.dot_general(a, b, (((0,), (0,)), ((), ())), preferred_element_type=F32)


def _split_bf16(x, n):
    parts, r = [], x
    for i in range(n):
        p = r.astype(BF16)
        parts.append(p)
        if i + 1 < n:
            r = r - p.astype(F32)
    return parts


def _sel_dot(m01, x, n):
    parts = _split_bf16(x, n)
    blocks = []
    for r0 in range(0, x.shape[0], CONST_ROWS):
        out = None
        for p in parts:
            t = _dot(m01, p[r0:r0 + CONST_ROWS])
            out = t if out is None else out + t
        blocks.append(out)
    return blocks[0] if len(blocks) == 1 else jnp.concatenate(blocks, axis=0)


def _dot_sel(x, m01, n):
    out = None
    for p in _split_bf16(x, n):
        t = _dot(p, m01)
        out = t if out is None else out + t
    return out


def _iota(shape, dim):
    return lax.broadcasted_iota(jnp.int32, shape, dim)


def _bd_inverse_all(ps, eye_bd):
    xs = [eye_bd + p for p in ps]
    ps = [_dot(p, p) for p in ps]
    for _ in range(LEVELS - 2):
        r = [_dot(p, jnp.concatenate([x, p], axis=1)) for p, x in zip(ps, xs)]
        xs = [x + t[:, 0:LANES] for x, t in zip(xs, r)]
        ps = [t[:, LANES:] for t in r]
    return [x + _dot(p, x) for p, x in zip(ps, xs)]


def _pair_masks():
    i = _iota((CHUNK, LANES), 0)
    c = _iota((CHUNK, LANES), 1)
    cm = c & (HEAD_DIM - 1)
    r = _iota((LANES, LANES), 0)
    l = _iota((LANES, LANES), 1)
    return dict(lo=c < HEAD_DIM, causal=cm <= i, strict=cm < i,
                bd=(r // HEAD_DIM) == (l // HEAD_DIM), eye=jnp.where(r == l, 1.0, 0.0))


def _stack(x, lo):
    return jnp.concatenate([jnp.where(lo, x, 0.0), jnp.where(lo, 0.0, x)], axis=0)


def _pair_tiles():
    return [slice(p * LANES, (p + 1) * LANES) for p in range(PAIRS)]


def _chunk_rows(c):
    return slice(c * CHUNK, (c + 1) * CHUNK)


def _scan_chunk(qm_ref, n_ref, s_ref, o_ref, c):
    rows = _chunk_rows(c)
    out = [_dot(qm_ref[c, p], s_ref[p]) for p in range(PAIRS)]
    for p, tile in enumerate(_pair_tiles()):
        s_ref[p] = out[p][CHUNK:] + n_ref[c, p]
        o_ref[rows, tile] = o_ref[rows, tile] + out[p][0:CHUNK]


def _run_chunks(tb, prep, scan_chunk, finish):
    chunks = list(range(tb // CHUNK))
    groups = [chunks[g:g + PREP_CHUNKS] for g in range(0, len(chunks), PREP_CHUNKS)]
    per_block = CONST_ROWS // CHUNK
    prep(groups[0])
    for g, cs in enumerate(groups):
        if g + 1 < len(groups):
            prep(groups[g + 1])
        for c in cs:
            scan_chunk(c)
            if (c + 1) % per_block == 0:
                finish(slice((c + 1) * CHUNK - CONST_ROWS, (c + 1) * CHUNK))


def _silu(x):
    return x * jax.nn.sigmoid(x)


def _norm_proj(x_ref, nw_ref, w_ref):
    x = x_ref[...]
    ms = jnp.mean(x * x, axis=-1, keepdims=True)
    h = (x * lax.rsqrt(ms + NORM_EPS)) * nw_ref[...]
    return _dot(h.astype(BF16), w_ref[...])


def _np_head_expand(offset):
    r = np.arange(LANES)[:, None]
    c = np.arange(WIDTH)[None, :]
    return r == c // HEAD_DIM + offset


def _np_head_reduce():
    return _np_head_expand(0).T


def _np_chunk_cumsum(n):
    r = np.arange(n)[:, None]
    c = np.arange(n)[None, :]
    return (r // CHUNK == c // CHUNK) & (c <= r)


def _np_level_matrix(n, s):
    r = np.arange(n)[:, None]
    c = np.arange(n)[None, :]
    tr, tc = r % CHUNK, c % CHUNK
    mid = (tr // (2 * s)) * (2 * s) + s - 1
    upper = (tr // s) % 2 == 1
    m = (upper & (tc > mid) & (tc <= tr)) | (~upper & (tc > tr) & (tc <= mid))
    return (r // CHUNK == c // CHUNK) & m


def _const(mask):
    return jnp.asarray(np.asarray(mask, np.float32), BF16)


def _gdn_kernel(x_ref, w_ref, nw_ref, conv_ref, alog_ref, dtb_ref, gnw_ref, cum_ref, red_ref, expb_ref,
                expg_ref, y_ref,
                s_ref, ext_ref, q_ref, k_ref, kb_ref, vb_ref, kbe_ref, qd_ref, gcf_ref, gcs_ref,
                qm_ref, n_ref, o_ref):
    tb = x_ref.shape[0]

    @pl.when(pl.program_id(0) == 0)
    def _():
        s_ref[...] = jnp.zeros_like(s_ref)
        ext_ref[0:SUBLANES, :] = jnp.zeros((SUBLANES, 3 * WIDTH), F32)

    p = _norm_proj(x_ref, nw_ref, w_ref)
    gate = p[:, 3 * WIDTH + 2 * HEADS:4 * WIDTH + 2 * HEADS]
    small = p[:, 3 * WIDTH:3 * WIDTH + LANES]

    ext_ref[SUBLANES:SUBLANES + tb, :] = p[:, 0:3 * WIDTH]
    conv = None
    for j in range(CONV_WIDTH):
        t = conv_ref[j:j + 1, :] * ext_ref[pl.ds(SUBLANES - (CONV_WIDTH - 1) + j, tb), :]
        conv = t if conv is None else conv + t
    ext_ref[0:SUBLANES, :] = ext_ref[tb:tb + SUBLANES, :]
    qkv = _silu(conv)
    q, k, v = qkv[:, 0:WIDTH], qkv[:, WIDTH:2 * WIDTH], qkv[:, 2 * WIDTH:3 * WIDTH]

    red = red_ref[...]
    exp_b = expb_ref[...]
    q_rs = lax.rsqrt(_dot_sel(q * q, red, SUM_PASSES) + L2_EPS)
    k_rs = lax.rsqrt(_dot_sel(k * k, red, SUM_PASSES) + L2_EPS)
    qn = (q * _dot_sel(q_rs, exp_b, EXPAND_PASSES)) * (HEAD_DIM ** -0.5)
    kn = k * _dot_sel(k_rs, exp_b, EXPAND_PASSES)

    beta = _dot_sel(jax.nn.sigmoid(small), exp_b, EXPAND_PASSES)
    g_small = -jnp.exp(alog_ref[...]) * jax.nn.softplus(small + dtb_ref[...])
    gc_small = _sel_dot(cum_ref[...], g_small, GDN_DECAY_PASSES)
    gc_full = _dot_sel(gc_small, expg_ref[...], GDN_DECAY_PASSES)
    egc = jnp.exp(gc_full)
    kb = kn * beta

    q_ref[...] = qn
    k_ref[...] = kn
    kb_ref[...] = kb
    vb_ref[...] = v * beta
    kbe_ref[...] = kb * egc
    qd_ref[...] = qn * egc
    gcf_ref[...] = gc_full
    gcs_ref[...] = gc_small

    mk = _pair_masks()
    lo, causal, strict, bd, eye = mk["lo"], mk["causal"], mk["strict"], mk["bd"], mk["eye"]
    tiles = _pair_tiles()

    def prep(cs):
        rws = [_chunk_rows(c) for c in cs]
        gc_c = [gcf_ref[rows, :] for rows in rws]
        gl = [gcf_ref[(c + 1) * CHUNK - 1:(c + 1) * CHUNK, :] for c in cs]
        kd_c = [k_ref[rows, :] * jnp.exp(g1 - g2) for rows, g1, g2 in zip(rws, gl, gc_c)]
        last = [jnp.exp(g1) for g1 in gl]
        g_rows = [gcs_ref[rows, :].T for rows in rws]
        items = [(j, p) for j in range(len(cs)) for p in range(PAIRS)]
        ak = [_dot_nt(jnp.concatenate([kb_ref[rws[j], tiles[p]], q_ref[rws[j], tiles[p]]], axis=0),
                      _stack(k_ref[rws[j], tiles[p]], lo)) for j, p in items]
        g_row = [jnp.concatenate([g_rows[j][HEADS + 2 * p:HEADS + 2 * p + 1, :],
                                  g_rows[j][HEADS + 2 * p + 1:HEADS + 2 * p + 2, :]], axis=1)
                 for j, p in items]
        dm = [jnp.where(causal, jnp.exp(jnp.where(causal, gc_c[j][:, tiles[p]] - g, 0.0)), 0.0)
              for (j, p), g in zip(items, g_row)]
        qk = [t[CHUNK:] * d for t, d in zip(ak, dm)]
        tm = _bd_inverse_all([_stack(jnp.where(strict, -(t[0:CHUNK] * d), 0.0), lo)
                              for t, d in zip(ak, dm)], eye)
        uw = [_dot(t, jnp.concatenate([_stack(vb_ref[rws[j], tiles[p]], lo),
                                       _stack(kbe_ref[rws[j], tiles[p]], lo)], axis=1))
              for (j, p), t in zip(items, tm)]
        t1 = [_dot(a, b) for a, b in zip(qk, uw)]
        t2 = [_dot_tn(kd_c[j][:, tiles[p]], b[0:CHUNK] + b[CHUNK:])
              for (j, p), b in zip(items, uw)]
        for n, (j, p) in enumerate(items):
            qm_ref[cs[j], p, 0:CHUNK, :] = qd_ref[rws[j], tiles[p]] - t1[n][:, LANES:]
            qm_ref[cs[j], p, CHUNK:, :] = eye * last[j][:, tiles[p]] - jnp.where(bd, t2[n][:, LANES:], 0.0)
            n_ref[cs[j], p] = jnp.where(bd, t2[n][:, 0:LANES], 0.0)
            o_ref[rws[j], tiles[p]] = t1[n][:, 0:LANES]

    def finish(rows):
        o = o_ref[rows, :]
        rs = lax.rsqrt(_dot_sel(o * o, red_ref[...], SUM_PASSES) * (1.0 / HEAD_DIM) + NORM_EPS)
        y_ref[rows, :] = (o * _dot_sel(rs, expb_ref[...], EXPAND_PASSES)) * gnw_ref[...] * _silu(gate[rows, :])

    _run_chunks(tb, prep, functools.partial(_scan_chunk, qm_ref, n_ref, s_ref, o_ref), finish)


def _rwkv_kernel(x_ref, w_ref, nw_ref, mu_ref, w0_ref, wup_ref, a0_ref, aup_ref, kk_ref, ka_ref,
                 rk_ref, lnw_ref, lnb_ref, cum_ref, red_ref, expd_ref, y_ref,
                 s_ref, ext_ref, qt_ref, rt_ref, ph_ref, kh_ref, pa_ref, k2_ref, v_ref, b_ref,
                 qm_ref, n_ref, o_ref):
    tb = x_ref.shape[0]

    @pl.when(pl.program_id(0) == 0)
    def _():
        s_ref[...] = jnp.zeros_like(s_ref)
        ext_ref[0:SUBLANES, :] = jnp.zeros((SUBLANES, RWKV_PROJ), F32)

    p = _norm_proj(x_ref, nw_ref, w_ref)
    ext_ref[SUBLANES:SUBLANES + tb, :] = p
    prev = ext_ref[pl.ds(SUBLANES - 1, tb), :]
    ext_ref[0:SUBLANES, :] = ext_ref[tb:tb + SUBLANES, :]
    p = p + mu_ref[...] * (prev - p)
    r, k, v = p[:, 0:WIDTH], p[:, WIDTH:2 * WIDTH], p[:, 2 * WIDTH:3 * WIDTH]
    wd = p[:, 3 * WIDTH:3 * WIDTH + LORA]
    ad = p[:, 3 * WIDTH + LORA:3 * WIDTH + 2 * LORA]
    gate = p[:, 3 * WIDTH + 2 * LORA:]

    w_raw = -jax.nn.softplus(-(w0_ref[...] + _dot(jnp.tanh(wd), wup_ref[...]))) - 0.5
    logw = -jnp.exp(w_raw)
    a = jax.nn.sigmoid(a0_ref[...] + _dot(ad, aup_ref[...]))

    red = red_ref[...]
    expd = expd_ref[...]
    kkp = k * kk_ref[...]
    kk = kkp * _dot_sel(lax.rsqrt(_dot_sel(kkp * kkp, red, SUM_PASSES) + L2_EPS), expd, EXPAND_PASSES)
    k2 = k * (1.0 + (a - 1.0) * ka_ref[...])
    pa = kk * a

    b = _sel_dot(cum_ref[...], logw, CUMSUM_PASSES)
    enb = jnp.exp(-b)
    qt_ref[...] = -kk * jnp.exp(b - logw)
    rt_ref[...] = r * jnp.exp(b)
    ph_ref[...] = pa * enb
    kh_ref[...] = k2 * enb
    pa_ref[...] = pa
    k2_ref[...] = k2
    v_ref[...] = v
    b_ref[...] = b

    mk = _pair_masks()
    lo, causal, strict, bd, eye = mk["lo"], mk["causal"], mk["strict"], mk["bd"], mk["eye"]
    tiles = _pair_tiles()

    def prep(cs):
        rws = [_chunk_rows(c) for c in cs]
        bl = [b_ref[(c + 1) * CHUNK - 1:(c + 1) * CHUNK, :] for c in cs]
        ebl = [jnp.exp(b1 - b_ref[rows, :]) for b1, rows in zip(bl, rws)]
        pt_c = [pa_ref[rows, :] * e for rows, e in zip(rws, ebl)]
        kt_c = [k2_ref[rows, :] * e for rows, e in zip(rws, ebl)]
        wl = [jnp.exp(b1) for b1 in bl]
        items = [(j, p) for j in range(len(cs)) for p in range(PAIRS)]
        qt = [qt_ref[rws[j], tiles[p]] for j, p in items]
        rt = [rt_ref[rws[j], tiles[p]] for j, p in items]
        v_t = [v_ref[rws[j], tiles[p]] for j, p in items]
        qr = [jnp.concatenate([a, b], axis=0) for a, b in zip(qt, rt)]
        aap = [_dot_nt(t, _stack(ph_ref[rws[j], tiles[p]], lo)) for (j, p), t in zip(items, qr)]
        aak = [_dot_nt(t, _stack(kh_ref[rws[j], tiles[p]], lo)) for (j, p), t in zip(items, qr)]
        av = [_dot(jnp.concatenate([jnp.where(strict, t[0:CHUNK], 0.0), jnp.where(causal, t[CHUNK:], 0.0)],
                                   axis=0), _stack(v, lo)) for t, v in zip(aak, v_t)]
        tinv = _bd_inverse_all([_stack(jnp.where(strict, t[0:CHUNK], 0.0), lo) for t in aap], eye)
        qu = [_dot(t, jnp.concatenate([_stack(a, lo), _stack(b[0:CHUNK], lo)], axis=1))
              for t, a, b in zip(tinv, qt, av)]
        t1 = [_dot(jnp.where(causal, t[CHUNK:], 0.0), y) for t, y in zip(aap, qu)]
        qu_t = [y[0:CHUNK] + y[CHUNK:] for y in qu]
        t2 = [_dot_tn(pt_c[j][:, tiles[p]], y[:, 0:LANES]) for (j, p), y in zip(items, qu_t)]
        t3 = [_dot_tn(jnp.concatenate([pt_c[j][:, tiles[p]], kt_c[j][:, tiles[p]]], axis=0),
                      jnp.concatenate([y[:, LANES:], v], axis=0))
              for (j, p), y, v in zip(items, qu_t, v_t)]
        for n, (j, p) in enumerate(items):
            qm_ref[cs[j], p, 0:CHUNK, :] = rt[n] + t1[n][:, 0:LANES]
            qm_ref[cs[j], p, CHUNK:, :] = eye * wl[j][:, tiles[p]] + jnp.where(bd, t2[n], 0.0)
            n_ref[cs[j], p] = jnp.where(bd, t3[n], 0.0)
            o_ref[rws[j], tiles[p]] = t1[n][:, LANES:] + av[n][CHUNK:]

    def finish(rows):
        o = o_ref[rows, :]
        inv_d = 1.0 / HEAD_DIM
        mean = _dot_sel(_dot_sel(o, red, SUM_PASSES) * inv_d, expd, EXPAND_PASSES)
        d = o - mean
        var = _dot_sel(d * d, red, SUM_PASSES) * inv_d
        on = d * _dot_sel(lax.rsqrt(var + RWKV_GN_EPS), expd, EXPAND_PASSES)
        on = on * lnw_ref[...] + lnb_ref[...]
        rkk = r[rows, :] * k2[rows, :] * rk_ref[...]
        bonus = _dot_sel(_dot_sel(rkk, red, SUM_PASSES), expd, EXPAND_PASSES) * v[rows, :]
        y_ref[rows, :] = (on + bonus) * _silu(gate[rows, :])

    _run_chunks(tb, prep, functools.partial(_scan_chunk, qm_ref, n_ref, s_ref, o_ref), finish)


def _hgrn_kernel(x_ref, w_ref, nw_ref, lbraw_ref, hnw_ref, cum_ref, lvl_ref, red_ref, expd_ref, y_ref,
                 s_ref, kin_ref, v_ref, b_ref, qe_ref, zl_ref, o_ref, *, layer, depth):
    tb = x_ref.shape[0]

    @pl.when(pl.program_id(0) == 0)
    def _():
        s_ref[...] = jnp.zeros_like(s_ref)

    lbs = [lbraw_ref[j:j + 1, :] for j in range(depth)]
    mx = lbs[0]
    for t in lbs[1:]:
        mx = jnp.maximum(mx, t)
    es = [jnp.exp(t - mx) for t in lbs]
    tot = es[0]
    for t in es[1:]:
        tot = tot + t
    soft = [t / tot for t in es]
    lb = soft[0]
    for t in soft[1:layer + 1]:
        lb = lb + t
    lb = lb - soft[0]

    p = _norm_proj(x_ref, nw_ref, w_ref)
    q = _silu(p[:, 0:WIDTH])
    f = p[:, WIDTH:2 * WIDTH]
    v = p[:, 2 * WIDTH:3 * WIDTH]
    gate = p[:, 3 * WIDTH:]
    log_g = jax.nn.log_sigmoid(f) + jnp.log1p(lb * jnp.exp(-f))
    kin = (1.0 - lb) * jax.nn.sigmoid(-f)

    b = _sel_dot(cum_ref[...], log_g, CUMSUM_PASSES)
    kin_ref[...] = kin
    v_ref[...] = v
    b_ref[...] = b
    qe_ref[...] = q * jnp.exp(b)
    tr = _iota((tb, WIDTH), 0)
    for l in range(LEVELS):
        upper = (tr & (1 << l)) != 0
        s = 1 << l
        if s == 1:
            d = jnp.where(upper, log_g, 0.0)
        elif 2 * s >= SUBLANES:
            b3 = b.reshape(tb // (2 * s), 2 * s, WIDTH)
            mid = jnp.broadcast_to(b3[:, s - 1:s, :], b3.shape).reshape(tb, WIDTH)
            d = jnp.where(upper, b - mid, mid - b)
        else:
            d = _sel_dot(lvl_ref[MATMUL_LEVELS.index(l)], log_g, CUMSUM_PASSES)
        zl_ref[l] = jnp.where(upper, q, kin) * jnp.exp(d)

    mk = _pair_masks()
    lo, bd = mk["lo"], mk["bd"]
    kj = _iota((LANES, CHUNK), 0) & (CHUNK - 1)
    qi = _iota((LANES, CHUNK), 1)
    lvl_masks_t = [((qi >> (l + 1)) == (kj >> (l + 1))) & ((qi & (1 << l)) != 0) & ((kj & (1 << l)) == 0)
                   for l in range(LEVELS)]
    tiles = _pair_tiles()

    def finish(rows):
        dterm = _dot_sel(_dot_sel(q[rows, :] * kin[rows, :], red_ref[...], SUM_PASSES), expd_ref[...],
                         EXPAND_PASSES) * v[rows, :]
        o = o_ref[rows, :] + dterm
        ms = jnp.mean(o * o, axis=-1, keepdims=True)
        y_ref[rows, :] = (o * lax.rsqrt(ms + NORM_EPS)) * hnw_ref[...] * _silu(gate[rows, :])

    st = [s_ref[p] for p in range(PAIRS)]
    per_block = CONST_ROWS // CHUNK
    for c in range(tb // CHUNK):
        rows = _chunk_rows(c)
        bl = b_ref[(c + 1) * CHUNK - 1:(c + 1) * CHUNK, :]
        kt_c = kin_ref[rows, :] * jnp.exp(bl - b_ref[rows, :])
        wl = jnp.exp(bl)
        att = [None] * PAIRS
        for l in range(LEVELS):
            for p in range(PAIRS):
                z = zl_ref[l, rows, tiles[p]]
                t = jnp.where(lvl_masks_t[l], _dot_nt(_stack(z, lo), z), 0.0)
                att[p] = t if att[p] is None else att[p] + t
        v_t = [v_ref[rows, tiles[p]] for p in range(PAIRS)]
        oi = [_dot_tn(a, _stack(vt, lo)) for a, vt in zip(att, v_t)]
        sn = [jnp.where(bd, _dot_tn(vt, kt_c[:, tiles[p]]), 0.0) for p, vt in enumerate(v_t)]
        for p in range(PAIRS):
            o_ref[rows, tiles[p]] = oi[p] + _dot_nt(qe_ref[rows, tiles[p]], st[p])
            st[p] = st[p] * wl[:, tiles[p]] + sn[p]
        if (c + 1) % per_block == 0:
            finish(slice((c + 1) * CHUNK - CONST_ROWS, (c + 1) * CHUNK))
    for p in range(PAIRS):
        s_ref[p] = st[p]


def _outproj_kernel(x_ref, ya_ref, yb_ref, yc_ref, w_ref, fnw_ref, o_ref, *, final):
    acc = x_ref[...]
    for i, y_ref in enumerate((ya_ref, yb_ref, yc_ref)):
        acc = acc + _dot(y_ref[...].astype(BF16), w_ref[i * WIDTH:(i + 1) * WIDTH, :])
    if final:
        ms = jnp.mean(acc * acc, axis=-1, keepdims=True)
        acc = (acc * lax.rsqrt(ms + NORM_EPS)) * fnw_ref[...]
    o_ref[...] = acc


def _full(shape):
    nd = len(shape)
    return pl.BlockSpec(shape, lambda i, _nd=nd: (0,) * _nd)


def _rows(tb, width):
    return pl.BlockSpec((tb, width), lambda i: (i, 0))


def _mixer_call(body, name, x2, w, layer, cols, params, scratch):
    t, d = x2.shape
    tb = TIME_BLOCK
    w_spec = pl.BlockSpec((None, d, cols), lambda i: (layer, 0, 0))
    return pl.pallas_call(
        body,
        grid=(t // tb,),
        in_specs=[_rows(tb, d), w_spec] + [_full(a.shape) for a in params],
        out_specs=_rows(tb, WIDTH),
        out_shape=jax.ShapeDtypeStruct((t, WIDTH), F32),
        scratch_shapes=scratch,
        compiler_params=pltpu.CompilerParams(dimension_semantics=("arbitrary",),
                                             vmem_limit_bytes=VMEM_LIMIT),
        name=name,
    )(x2, w, *params)


def _slab(tb):
    return pltpu.VMEM((tb, WIDTH), F32)


def _state():
    return pltpu.VMEM((PAIRS, LANES, LANES), F32)


def _folded(nc):
    return [pltpu.VMEM((nc, PAIRS, CHUNK + LANES, LANES), F32), pltpu.VMEM((nc, PAIRS, LANES, LANES), F32)]


def _row(a):
    return a.reshape(1, -1).astype(F32)


def _lane_pad(a, offset):
    return jnp.zeros((1, LANES), F32).at[0, offset:offset + HEADS].set(a.astype(F32))


def kernel(x, norm_w, w_in, gdn_conv_w, gdn_a_log, gdn_dt_bias, gdn_norm_w, rwkv_mu, rwkv_w0, rwkv_w_up, rwkv_a0, rwkv_a_up, rwkv_k_k, rwkv_k_a, rwkv_r_k, rwkv_ln_w, rwkv_ln_b, hgrn_lower_bounds, hgrn_norm_w, w_out, final_norm_w):
    bsz, seq, d = x.shape
    depth = norm_w.shape[0]
    assert bsz == 1 and seq % TIME_BLOCK == 0 and seq % OUT_BLOCK == 0
    tb = TIME_BLOCK
    nc = tb // CHUNK
    x2 = x.reshape(seq, d)

    gdn_cols = 3 * WIDTH + 2 * HEADS + WIDTH
    rwkv_off = gdn_cols
    hgrn_off = gdn_cols + RWKV_PROJ

    cum = _const(_np_chunk_cumsum(CONST_ROWS))
    red = _const(_np_head_reduce())
    exp0 = _const(_np_head_expand(0))
    exp8 = _const(_np_head_expand(HEADS))
    lvl = _const(np.stack([_np_level_matrix(CONST_ROWS, 1 << l) for l in MATMUL_LEVELS]))

    wb = w_in.astype(BF16)
    w_rwkv = wb[:, :, rwkv_off:rwkv_off + RWKV_PROJ]
    w_hgrn = wb[:, :, hgrn_off:hgrn_off + HGRN_PROJ]
    w_out_b = w_out.astype(BF16)

    for l in range(depth):
        nw = _row(norm_w[l])

        y_a = _mixer_call(
            _gdn_kernel, f"gdn_{l}", x2, wb, l, PROJ_BLOCK,
            [nw, gdn_conv_w[l].astype(F32), _lane_pad(gdn_a_log[l], HEADS),
             _lane_pad(gdn_dt_bias[l], HEADS), _row(jnp.tile(gdn_norm_w[l], HEADS)),
             cum, red, exp0, exp8],
            [_state(), pltpu.VMEM((tb + SUBLANES, 3 * WIDTH), F32)] + [_slab(tb)] * 7
            + [pltpu.VMEM((tb, LANES), F32)] + _folded(nc) + [_slab(tb)])

        y_b = _mixer_call(
            _rwkv_kernel, f"rwkv_{l}", x2, w_rwkv, l, RWKV_PROJ,
            [nw, _row(rwkv_mu[l]), _row(rwkv_w0[l]), rwkv_w_up[l].astype(F32),
             _row(rwkv_a0[l]), rwkv_a_up[l].astype(F32), _row(rwkv_k_k[l]), _row(rwkv_k_a[l]),
             _row(rwkv_r_k[l]), _row(rwkv_ln_w[l]), _row(rwkv_ln_b[l]), cum, red, exp0],
            [_state(), pltpu.VMEM((tb + SUBLANES, RWKV_PROJ), F32)] + [_slab(tb)] * 8
            + _folded(nc) + [_slab(tb)])

        y_c = _mixer_call(
            functools.partial(_hgrn_kernel, layer=l, depth=depth), f"hgrn_{l}", x2, w_hgrn, l, HGRN_PROJ,
            [nw, hgrn_lower_bounds.astype(F32), _row(hgrn_norm_w[l]), cum, lvl, red, exp0],
            [_state()] + [_slab(tb)] * 4 + [pltpu.VMEM((LEVELS, tb, WIDTH), F32), _slab(tb)])

        final = l == depth - 1
        x2 = pl.pallas_call(
            functools.partial(_outproj_kernel, final=final),
            grid=(seq // OUT_BLOCK,),
            in_specs=[_rows(OUT_BLOCK, d)] + [_rows(OUT_BLOCK, WIDTH)] * 3
            + [_full((3 * WIDTH, d)), _full((1, d))],
            out_specs=_rows(OUT_BLOCK, d),
            out_shape=jax.ShapeDtypeStruct((seq, d), F32),
            compiler_params=pltpu.CompilerParams(dimension_semantics=("arbitrary",),
                                                 vmem_limit_bytes=VMEM_LIMIT),
            name=f"outproj_{l}",
        )(x2, y_a, y_b, y_c, w_out_b[l], _row(final_norm_w))

    return x2.reshape(bsz, seq, d)
```
